```python
import jax
import jax.numpy as jnp
from jax import lax
import numpy as np

D_MODEL = 1024
BATCH = 2
SEQ = 8192
DEPTH = 4
DEC_BATCH = 128
DEC_SEQ = 4
PAST_LEN = 2048
PAGE_SIZE = 128

N_MIXERS = 3
LAYER_MIXERS = tuple(i % N_MIXERS for i in range(DEPTH))
LAYER_SLOT = tuple(LAYER_MIXERS[:i].count(LAYER_MIXERS[i]) for i in range(DEPTH))
N_A_LAYERS = LAYER_MIXERS.count(0)
N_B_LAYERS = LAYER_MIXERS.count(1)
N_C_LAYERS = LAYER_MIXERS.count(2)

D_FF = 2816
PLE_DIM = 256
RMS_EPS = 1e-6
LN_EPS = 1e-5

CHUNK = 128
D_A = D_MODEL
A_GROUPS = 8
A_GROUP_DIM = D_A // A_GROUPS

N_HEADS = 8
HEAD_DIM = D_MODEL // N_HEADS
ROT_DIM = HEAD_DIM // 4
ROPE_THETA = 500000.0
MOBA_BLOCK = 256
MOBA_TOP_K = 3
Q_BLOCK = 128

POOL_WINDOWS = (2, 4, 8, 16)
POOL_GROUPS = len(POOL_WINDOWS)
D_C = D_MODEL
C_GROUP_DIM = D_C // POOL_GROUPS
POOL_HIST = max(POOL_WINDOWS) - 1

kernel_name = 'hybrid_gmlp_moba_pool_decoder_step'


def _rms_norm(x, g):
    x32 = x.astype(jnp.float32)
    y = x32 * lax.rsqrt(jnp.mean(x32 * x32, axis=-1, keepdims=True) + RMS_EPS)
    return (y * g.astype(jnp.float32)).astype(x.dtype)


def _half_ffn(x, g, w_gate, w_up, w_down):
    h = _rms_norm(x, g)
    return x + 0.5 * ((jax.nn.silu(h @ w_gate) * (h @ w_up)) @ w_down)


def _ple(x, g, w_gate, w_proj, p):
    gate = jax.nn.sigmoid(_rms_norm(x, g) @ w_gate)
    return x + gate * (p @ w_proj)


def _chunk_mlp(h, w_in, ln_g, ln_b, w_s, b_s, w_out):
    n, length, _ = h.shape
    z = jax.nn.gelu(h @ w_in, approximate=False)
    u, v = z[..., :D_A], z[..., D_A:]
    v32 = v.astype(jnp.float32)
    mu = jnp.mean(v32, axis=-1, keepdims=True)
    var = jnp.mean(jnp.square(v32 - mu), axis=-1, keepdims=True)
    v = ((v32 - mu) * lax.rsqrt(var + LN_EPS) * ln_g.astype(jnp.float32) + ln_b.astype(jnp.float32)).astype(h.dtype)
    rows = min(length, CHUNK)
    n_chunks = length // rows
    causal = jnp.tril(jnp.ones((rows, rows), dtype=bool))
    w = jnp.where(causal[None], w_s[:, :rows, :rows], 0)
    vg = v.reshape(n, n_chunks, rows, A_GROUPS, A_GROUP_DIM)
    mixed = jnp.einsum('gts,ncsgk->nctgk', w, vg) + b_s[:, :rows].T[None, None, :, :, None]
    y = (u * mixed.reshape(n, length, D_A)) @ w_out
    return y, v


def _pool_mixer(h, hist, w_in, w_grp, scale, w_out):
    n, length, _ = h.shape
    u = h @ w_in
    ext = u if hist is None else jnp.concatenate([hist, u], axis=1)
    n_prev = ext.shape[1] - length
    e32 = ext.astype(jnp.float32)
    csum = jnp.cumsum(e32, axis=1)
    count = jnp.arange(1, ext.shape[1] + 1, dtype=jnp.float32)[None, :, None]
    parts = []
    for g, win in enumerate(POOL_WINDOWS):
        sl = slice(g * C_GROUP_DIM, (g + 1) * C_GROUP_DIM)
        c = csum[..., sl]
        c_prev = jnp.pad(c[:, :-win], ((0, 0), (win, 0), (0, 0)))
        parts.append((c - c_prev) / jnp.minimum(count, win) - e32[..., sl])
    pooled = jnp.concatenate(parts, axis=-1)[:, n_prev:].astype(h.dtype)
    mixed = jnp.einsum('nlgc,gcd->nlgd', pooled.reshape(n, length, POOL_GROUPS, C_GROUP_DIM), w_grp)
    y = (mixed.reshape(n, length, D_C) * scale) @ w_out
    return y, ext[:, -POOL_HIST:]


def _partial_rope(x, pos):
    half = ROT_DIM // 2
    inv_freq = ROPE_THETA ** (-jnp.arange(half, dtype=jnp.float32) / half)
    ang = pos.astype(jnp.float32)[:, None] * inv_freq[None, :]
    cos = jnp.cos(ang)[:, None, :]
    sin = jnp.sin(ang)[:, None, :]
    xr = x[..., :ROT_DIM].astype(jnp.float32)
    x1, x2 = xr[..., :half], xr[..., half:]
    rot = jnp.concatenate([x1 * cos - x2 * sin, x2 * cos + x1 * sin], axis=-1).astype(x.dtype)
    return jnp.concatenate([rot, x[..., ROT_DIM:]], axis=-1)


def _qkv(h, w_qkv, pos):
    n, length, _ = h.shape
    qkv = (h @ w_qkv).reshape(n, length, 3, N_HEADS, HEAD_DIM)
    return _partial_rope(qkv[:, :, 0], pos), _partial_rope(qkv[:, :, 1], pos), qkv[:, :, 2]


def _attend(q, k_sel, v_sel, m_sel, k_loc, v_loc, m_loc):
    scale = HEAD_DIM ** -0.5
    s_loc = jnp.where(m_loc, jnp.einsum('nqd,nkd->nqk', q, k_loc).astype(jnp.float32) * scale, -jnp.inf)
    if k_sel is None:
        p = jax.nn.softmax(s_loc, axis=-1).astype(v_loc.dtype)
        return jnp.einsum('nqk,nkd->nqd', p, v_loc)
    s_sel = jnp.einsum('nqd,nqkd->nqk', q, k_sel).astype(jnp.float32) * scale
    if m_sel is not None:
        s_sel = jnp.where(m_sel, s_sel, -jnp.inf)
    n_sel = k_sel.shape[2]
    p = jax.nn.softmax(jnp.concatenate([s_sel, s_loc], axis=-1), axis=-1).astype(v_loc.dtype)
    return (jnp.einsum('nqk,nqkd->nqd', p[..., :n_sel], v_sel)
            + jnp.einsum('nqk,nkd->nqd', p[..., n_sel:], v_loc))


def _moba_prompt(h, w_qkv, w_out):
    b, s, _ = h.shape
    pos = jnp.arange(s)
    q, k, v = _qkv(h, w_qkv, pos)
    nb = -(-s // MOBA_BLOCK)
    pad = nb * MOBA_BLOCK - s
    kp = jnp.pad(k, ((0, 0), (0, pad), (0, 0), (0, 0)))
    vp = jnp.pad(v, ((0, 0), (0, pad), (0, 0), (0, 0)))
    q_blk = pos // MOBA_BLOCK
    topk = min(MOBA_TOP_K, nb - 1)
    if topk > 0:
        k_mean = jnp.mean(kp.reshape(b, nb, MOBA_BLOCK, N_HEADS, HEAD_DIM).astype(jnp.float32), axis=2)
        gate = jnp.einsum('bshd,bnhd->bhsn', q.astype(jnp.float32), k_mean)
        gate = jnp.where(jnp.arange(nb)[None, :] < q_blk[:, None], gate, -jnp.inf)
        _, sel = lax.top_k(gate, topk)
        sel = sel.transpose(1, 0, 2, 3)
        valid = sel < q_blk[None, None, :, None]
    qh = q.transpose(2, 0, 1, 3)
    kh = kp.transpose(2, 0, 1, 3)
    vh = vp.transpose(2, 0, 1, 3)
    n_qb = s // Q_BLOCK
    bidx = jnp.arange(b)[:, None, None]

    def per_head(hh):
        q_h, k_h, v_h = qh[hh], kh[hh], vh[hh]
        kb_h = k_h.reshape(b, nb, MOBA_BLOCK, HEAD_DIM)
        vb_h = v_h.reshape(b, nb, MOBA_BLOCK, HEAD_DIM)
        if topk > 0:
            sel_h, valid_h = sel[hh], valid[hh]

        def per_qblock(i):
            q0 = i * Q_BLOCK
            qi = lax.dynamic_slice_in_dim(q_h, q0, Q_BLOCK, axis=1)
            own0 = (q0 // MOBA_BLOCK) * MOBA_BLOCK
            k_loc = lax.dynamic_slice_in_dim(k_h, own0, MOBA_BLOCK, axis=1)
            v_loc = lax.dynamic_slice_in_dim(v_h, own0, MOBA_BLOCK, axis=1)
            m_loc = (own0 + jnp.arange(MOBA_BLOCK))[None, :] <= (q0 + jnp.arange(Q_BLOCK))[:, None]
            k_sel = v_sel = m_sel = None
            if topk > 0:
                si = lax.dynamic_slice_in_dim(sel_h, q0, Q_BLOCK, axis=1)
                vi = lax.dynamic_slice_in_dim(valid_h, q0, Q_BLOCK, axis=1)
                k_sel = kb_h[bidx, si].reshape(b, Q_BLOCK, topk * MOBA_BLOCK, HEAD_DIM)
                v_sel = vb_h[bidx, si].reshape(b, Q_BLOCK, topk * MOBA_BLOCK, HEAD_DIM)
                m_sel = jnp.repeat(vi, MOBA_BLOCK, axis=-1)
            return _attend(qi, k_sel, v_sel, m_sel, k_loc, v_loc, m_loc)

        return lax.map(per_qblock, jnp.arange(n_qb))

    o = lax.map(per_head, jnp.arange(N_HEADS))
    y = o.transpose(2, 1, 3, 0, 4).reshape(b, s, N_HEADS * HEAD_DIM) @ w_out
    return y, k, v


def _moba_sample(h, cache_k, cache_v, layer, page_table, w_qkv, w_out):
    n, t, _ = h.shape
    n_pages = page_table.shape[1]
    past_len = n_pages * PAGE_SIZE
    q, k, v = _qkv(h, w_qkv, past_len + jnp.arange(t))
    ppb = MOBA_BLOCK // PAGE_SIZE
    own = past_len // MOBA_BLOCK
    r_pages = n_pages - own * ppb
    causal = jnp.tril(jnp.ones((t, t), dtype=bool))
    if r_pages > 0:
        own_pages = page_table[:, own * ppb:]
        k_own = cache_k[layer, own_pages].reshape(n, r_pages * PAGE_SIZE, N_HEADS, HEAD_DIM)
        v_own = cache_v[layer, own_pages].reshape(n, r_pages * PAGE_SIZE, N_HEADS, HEAD_DIM)
        k_loc = jnp.concatenate([k_own, k], axis=1)
        v_loc = jnp.concatenate([v_own, v], axis=1)
        m_loc = jnp.concatenate([jnp.ones((t, r_pages * PAGE_SIZE), dtype=bool), causal], axis=1)
    else:
        k_loc, v_loc, m_loc = k, v, causal
    topk = min(MOBA_TOP_K, own)
    if topk > 0:
        full = cache_k[layer, page_table[:, :own * ppb]]
        k_mean = jnp.mean(full.reshape(n, own, MOBA_BLOCK, N_HEADS, HEAD_DIM).astype(jnp.float32), axis=2)
        gate = jnp.einsum('nthd,nbhd->nhtb', q.astype(jnp.float32), k_mean)
        _, sel = lax.top_k(gate, topk)
        logical = sel[..., None] * ppb + jnp.arange(ppb)
        phys = page_table[jnp.arange(n)[:, None, None, None, None], logical].transpose(1, 0, 2, 3, 4)
    qh = q.transpose(2, 0, 1, 3)
    klh = k_loc.transpose(2, 0, 1, 3)
    vlh = v_loc.transpose(2, 0, 1, 3)

    def per_head(hh):
        k_sel = v_sel = None
        if topk > 0:
            ph = phys[hh]
            k_sel = cache_k[layer, ph, :, hh].reshape(n, t, topk * MOBA_BLOCK, HEAD_DIM)
            v_sel = cache_v[layer, ph, :, hh].reshape(n, t, topk * MOBA_BLOCK, HEAD_DIM)
        return _attend(qh[hh], k_sel, v_sel, None, klh[hh], vlh[hh], m_loc)

    o = lax.map(per_head, jnp.arange(N_HEADS))
    y = o.transpose(1, 2, 0, 3).reshape(n, t, N_HEADS * HEAD_DIM) @ w_out
    return y, k, v


def setup_inputs(seed: int = 0) -> dict:
    key = jax.random.key(seed)
    keys = iter(jax.random.split(key, 48))

    def nrm(shape, scale=1.0):
        return jax.random.normal(next(keys), shape, jnp.float32) * scale

    def gain(shape):
        return 1.0 + nrm(shape, 0.05)

    d = D_MODEL
    n_pages = PAST_LEN // PAGE_SIZE
    n_pool = (5 * DEC_BATCH * n_pages) // 4
    return {
        'x_prompt': nrm((BATCH, SEQ, d)),
        'x_sample': nrm((DEC_BATCH, DEC_SEQ, d)),
        'cache_k': nrm((N_B_LAYERS, n_pool, PAGE_SIZE, N_HEADS, HEAD_DIM)),
        'cache_v': nrm((N_B_LAYERS, n_pool, PAGE_SIZE, N_HEADS, HEAD_DIM)),
        'state_pool': nrm((N_C_LAYERS, DEC_BATCH, POOL_HIST, D_C)),
        'page_table': jax.random.permutation(next(keys), n_pool)[:DEC_BATCH * n_pages].reshape(DEC_BATCH, n_pages).astype(jnp.int32),
        'p_prompt': nrm((DEPTH, BATCH, SEQ, PLE_DIM)),
        'p_sample': nrm((DEPTH, DEC_BATCH, DEC_SEQ, PLE_DIM)),
        'norm_ffn1': gain((DEPTH, d)),
        'norm_mix': gain((DEPTH, d)),
        'norm_ffn2': gain((DEPTH, d)),
        'norm_ple': gain((DEPTH, d)),
        'norm_final': gain((d,)),
        'ffn1_w_gate': nrm((DEPTH, d, D_FF), d ** -0.5),
        'ffn1_w_up': nrm((DEPTH, d, D_FF), d ** -0.5),
        'ffn1_w_down': nrm((DEPTH, D_FF, d), D_FF ** -0.5),
        'ffn2_w_gate': nrm((DEPTH, d, D_FF), d ** -0.5),
        'ffn2_w_up': nrm((DEPTH, d, D_FF), d ** -0.5),
        'ffn2_w_down': nrm((DEPTH, D_FF, d), D_FF ** -0.5),
        'ple_w_gate': nrm((DEPTH, d, d), d ** -0.5),
        'ple_w_proj': nrm((DEPTH, PLE_DIM, d), PLE_DIM ** -0.5),
        'a_w_in': nrm((N_A_LAYERS, d, 2 * D_A), d ** -0.5),
        'a_ln_g': gain((N_A_LAYERS, D_A)),
        'a_ln_b': nrm((N_A_LAYERS, D_A), 0.02),
        'a_w_s': nrm((N_A_LAYERS, A_GROUPS, CHUNK, CHUNK), CHUNK ** -0.5),
        'a_b_s': gain((N_A_LAYERS, A_GROUPS, CHUNK)),
        'a_w_out': nrm((N_A_LAYERS, D_A, d), D_A ** -0.5),
        'b_w_qkv': nrm((N_B_LAYERS, d, 3 * N_HEADS * HEAD_DIM), d ** -0.5),
        'b_w_out': nrm((N_B_LAYERS, N_HEADS * HEAD_DIM, d), (N_HEADS * HEAD_DIM) ** -0.5),
        'c_w_in': nrm((N_C_LAYERS, d, D_C), d ** -0.5),
        'c_w_grp': nrm((N_C_LAYERS, POOL_GROUPS, C_GROUP_DIM, C_GROUP_DIM), C_GROUP_DIM ** -0.5),
        'c_scale': gain((N_C_LAYERS, D_C)),
        'c_w_out': nrm((N_C_LAYERS, D_C, d), D_C ** -0.5),
    }


def reference(x_prompt, x_sample, cache_k, cache_v, state_pool, page_table, p_prompt, p_sample,
              norm_ffn1, norm_mix, norm_ffn2, norm_ple, norm_final,
              ffn1_w_gate, ffn1_w_up, ffn1_w_down, ffn2_w_gate, ffn2_w_up, ffn2_w_down,
              ple_w_gate, ple_w_proj,
              a_w_in, a_ln_g, a_ln_b, a_w_s, a_b_s, a_w_out,
              b_w_qkv, b_w_out,
              c_w_in, c_w_grp, c_scale, c_w_out):
    y_p, y_s = x_prompt, x_sample
    k_p, v_p, k_s, v_s = [], [], [], []
    pool_p, pool_s, chunk_v = [], [], []
    for i in range(DEPTH):
        kind, j = LAYER_MIXERS[i], LAYER_SLOT[i]
        y_p = _half_ffn(y_p, norm_ffn1[i], ffn1_w_gate[i], ffn1_w_up[i], ffn1_w_down[i])
        y_s = _half_ffn(y_s, norm_ffn1[i], ffn1_w_gate[i], ffn1_w_up[i], ffn1_w_down[i])
        h_p = _rms_norm(y_p, norm_mix[i])
        h_s = _rms_norm(y_s, norm_mix[i])
        if kind == 0:
            m_p, _ = _chunk_mlp(h_p, a_w_in[j], a_ln_g[j], a_ln_b[j], a_w_s[j], a_b_s[j], a_w_out[j])
            m_s, v_rows = _chunk_mlp(h_s, a_w_in[j], a_ln_g[j], a_ln_b[j], a_w_s[j], a_b_s[j], a_w_out[j])
            chunk_v.append(v_rows)
        elif kind == 1:
            m_p, k_rows, v_rows = _moba_prompt(h_p, b_w_qkv[j], b_w_out[j])
            k_p.append(k_rows)
            v_p.append(v_rows)
            m_s, k_rows, v_rows = _moba_sample(h_s, cache_k, cache_v, j, page_table, b_w_qkv[j], b_w_out[j])
            k_s.append(k_rows)
            v_s.append(v_rows)
        else:
            m_p, hist = _pool_mixer(h_p, None, c_w_in[j], c_w_grp[j], c_scale[j], c_w_out[j])
            pool_p.append(hist)
            m_s, hist = _pool_mixer(h_s, state_pool[j], c_w_in[j], c_w_grp[j], c_scale[j], c_w_out[j])
            pool_s.append(hist)
        y_p = y_p + m_p
        y_s = y_s + m_s
        y_p = _half_ffn(y_p, norm_ffn2[i], ffn2_w_gate[i], ffn2_w_up[i], ffn2_w_down[i])
        y_s = _half_ffn(y_s, norm_ffn2[i], ffn2_w_gate[i], ffn2_w_up[i], ffn2_w_down[i])
        y_p = _ple(y_p, norm_ple[i], ple_w_gate[i], ple_w_proj[i], p_prompt[i])
        y_s = _ple(y_s, norm_ple[i], ple_w_gate[i], ple_w_proj[i], p_sample[i])
    y_prompt = _rms_norm(y_p, norm_final)
    y_sample = _rms_norm(y_s, norm_final)
    k_prompt = jnp.stack(k_p)
    v_prompt = jnp.stack(v_p)
    k_sample = jnp.stack(k_s)
    v_sample = jnp.stack(v_s)
    pool_prompt = jnp.stack(pool_p)
    pool_sample = jnp.stack(pool_s)
    chunk_v_sample = jnp.stack(chunk_v)
    return (y_prompt, y_sample, k_prompt, v_prompt, k_sample, v_sample, pool_prompt, pool_sample, chunk_v_sample)
```

```python
import functools
import math

import jax
import jax.numpy as jnp
from jax import lax
from jax.experimental import pallas as pl
from jax.experimental.pallas import tpu as pltpu

F32 = jnp.float32
BF16 = jnp.bfloat16

D_MODEL = 1024
DEPTH = 4
D_FF = 2816
PLE_DIM = 256
RMS_EPS = 1e-6
LN_EPS = 1e-5
CHUNK = 128
A_GROUPS = 8
N_HEADS = 8
HEAD_DIM = 128
ROT_DIM = 32
ROPE_THETA = 500000.0
MOBA_BLOCK = 256
MOBA_TOP_K = 3
PAGE_SIZE = 128
POOL_WINDOWS = (2, 4, 8, 16)
C_GROUP_DIM = 256
POOL_HIST = 15
LAYER_MIXERS = (0, 1, 2, 0)
LAYER_SLOT = (0, 0, 0, 1)

LANES = 128
SUBLANES = 8
VMEM_LIMIT = 56 * 1024 * 1024

ROW_TILE = 512
NEG_BIG = -1e30
FF_SPLIT = 2


def _dot(a, b):
    return jnp.dot(a, b, preferred_element_type=F32)


def _dot_t0(a, b):
    return lax.dot_general(a, b, (((0,), (0,)), ((), ())), preferred_element_type=F32)


def _dot_t1(a, b):
    return lax.dot_general(a, b, (((1,), (1,)), ((), ())), preferred_element_type=F32)


def _rms(x, g):
    return x * lax.rsqrt(jnp.mean(x * x, axis=-1, keepdims=True) + RMS_EPS) * g


def _split_bf16(x):
    hi = x.astype(BF16)
    lo = (x - hi.astype(F32)).astype(BF16)
    return hi, lo


def _const_spec(shape):
    nd = len(shape)
    return pl.BlockSpec(shape, lambda *_: (0,) * nd, pipeline_mode=pl.Buffered(1))


def _row_spec(width):
    return pl.BlockSpec((ROW_TILE, width), lambda i: (i, 0))


def _params(n_axes=1):
    return pltpu.CompilerParams(dimension_semantics=("arbitrary",) * n_axes,
                                vmem_limit_bytes=VMEM_LIMIT)


def _ffn_body(x_ref, g_ref, wg_ref, wu_ref, wd_ref, o_ref):
    x = x_ref[...]
    h = _rms(x, g_ref[...]).astype(BF16)
    width = D_FF // FF_SPLIT
    acc = None
    for c in range(FF_SPLIT):
        sl = slice(c * width, (c + 1) * width)
        gate = _dot(h, wg_ref[:, sl])
        up = _dot(h, wu_ref[:, sl])
        a = (gate * jax.nn.sigmoid(gate) * up).astype(BF16)
        part = _dot(a, wd_ref[sl, :])
        acc = part if acc is None else acc + part
    o_ref[...] = x + 0.5 * acc


def _ffn(x, g, wg, wu, wd):
    n = x.shape[0]
    return pl.pallas_call(
        _ffn_body,
        grid=(n // ROW_TILE,),
        in_specs=[_row_spec(D_MODEL), _const_spec((1, D_MODEL)),
                  _const_spec((D_MODEL, D_FF)), _const_spec((D_MODEL, D_FF)),
                  _const_spec((D_FF, D_MODEL))],
        out_specs=_row_spec(D_MODEL),
        out_shape=jax.ShapeDtypeStruct((n, D_MODEL), F32),
        compiler_params=_params(),
        name="ffn",
    )(x, g, wg, wu, wd)


def _ple_body(x_ref, g_ref, wgate_ref, p_ref, wproj_ref, o_ref):
    x = x_ref[...]
    h = _rms(x, g_ref[...]).astype(BF16)
    gate = jax.nn.sigmoid(_dot(h, wgate_ref[...]))
    o_ref[...] = x + gate * _dot(p_ref[...].astype(BF16), wproj_ref[...])


def _ple_final_body(n_prompt_tiles, x_ref, g_ref, wgate_ref, p_ref, wproj_ref, gf_ref, yp_ref, ys_ref):
    i = pl.program_id(0)
    x = x_ref[...]
    h = _rms(x, g_ref[...]).astype(BF16)
    gate = jax.nn.sigmoid(_dot(h, wgate_ref[...]))
    y = _rms(x + gate * _dot(p_ref[...].astype(BF16), wproj_ref[...]), gf_ref[...])

    @pl.when(i < n_prompt_tiles)
    def _():
        yp_ref[...] = y

    @pl.when(i >= n_prompt_tiles)
    def _():
        ys_ref[...] = y


def _ple(x, g, wgate, p, wproj):
    n = x.shape[0]
    return pl.pallas_call(
        _ple_body,
        grid=(n // ROW_TILE,),
        in_specs=[_row_spec(D_MODEL), _const_spec((1, D_MODEL)), _const_spec((D_MODEL, D_MODEL)),
                  _row_spec(PLE_DIM), _const_spec((PLE_DIM, D_MODEL))],
        out_specs=_row_spec(D_MODEL),
        out_shape=jax.ShapeDtypeStruct((n, D_MODEL), F32),
        compiler_params=_params(),
        name="ple",
    )(x, g, wgate, p, wproj)


def _ple_final(x, g, wgate, p, wproj, gf, n_prompt):
    n = x.shape[0]
    npt = n_prompt // ROW_TILE
    assert n - n_prompt == ROW_TILE
    return pl.pallas_call(
        functools.partial(_ple_final_body, npt),
        grid=(n // ROW_TILE,),
        in_specs=[_row_spec(D_MODEL), _const_spec((1, D_MODEL)), _const_spec((D_MODEL, D_MODEL)),
                  _row_spec(PLE_DIM), _const_spec((PLE_DIM, D_MODEL)), _const_spec((1, D_MODEL))],
        out_specs=[pl.BlockSpec((ROW_TILE, D_MODEL), lambda i: (jnp.minimum(i, npt - 1), 0)),
                   pl.BlockSpec((ROW_TILE, D_MODEL), lambda i: (0, 0))],
        out_shape=[jax.ShapeDtypeStruct((n_prompt, D_MODEL), F32),
                   jax.ShapeDtypeStruct((ROW_TILE, D_MODEL), F32)],
        compiler_params=_params(),
        name="ple_final",
    )(x, g, wgate, p, wproj, gf)


def _mixa_body(n_prompt_tiles, dec_batch, dec_seq,
               x_ref, g_ref, win_ref, lng_ref, lnb_ref, ws_ref, bexp_ref, wrow_ref, brow_ref, wout_ref,
               o_ref, vs_ref, v_scr, m_scr):
    i = pl.program_id(0)
    x = x_ref[...]
    h = _rms(x, g_ref[...]).astype(BF16)
    z = _dot(h, win_ref[...])
    z = 0.5 * z * (1.0 + lax.erf(z * (1.0 / math.sqrt(2.0))))
    u = z[:, :D_MODEL]
    v = z[:, D_MODEL:]
    mu = jnp.mean(v, axis=-1, keepdims=True)
    vc = v - mu
    var = jnp.mean(vc * vc, axis=-1, keepdims=True)
    v_scr[...] = vc * lax.rsqrt(var + LN_EPS) * lng_ref[...] + lnb_ref[...]
    n_chunks = ROW_TILE // CHUNK

    @pl.when(i < n_prompt_tiles)
    def _():
        rows = lax.broadcasted_iota(jnp.int32, (CHUNK, CHUNK), 0)
        cols = lax.broadcasted_iota(jnp.int32, (CHUNK, CHUNK), 1)
        causal = cols <= rows
        for g in range(A_GROUPS):
            gs = slice(g * LANES, (g + 1) * LANES)
            w = jnp.where(causal, ws_ref[g], 0.0).astype(BF16)
            rhs = jnp.concatenate([v_scr[c * CHUNK:(c + 1) * CHUNK, gs] for c in range(n_chunks)],
                                  axis=1).astype(BF16)
            mg = _dot(w, rhs)
            for c in range(n_chunks):
                m_scr[c * CHUNK:(c + 1) * CHUNK, gs] = mg[:, c * LANES:(c + 1) * LANES] + bexp_ref[:, gs]

    @pl.when(i >= n_prompt_tiles)
    def _():
        for t in range(dec_seq):
            acc = brow_ref[t:t + 1, :] + wrow_ref[t, 0:1, :] * v_scr[0:dec_batch, :]
            for s in range(1, t + 1):
                acc = acc + wrow_ref[t, s:s + 1, :] * v_scr[s * dec_batch:(s + 1) * dec_batch, :]
            m_scr[t * dec_batch:(t + 1) * dec_batch, :] = acc
        vs_ref[...] = v_scr[...]

    y = _dot((u * m_scr[...]).astype(BF16), wout_ref[...])
    o_ref[...] = x + y


def _mixer_a(x, g, w_in, ln_g, ln_b, w_s, bexp, wrow, brow, w_out, n_prompt, dec_batch, dec_seq):
    n = x.shape[0]
    assert n - n_prompt == ROW_TILE == dec_batch * dec_seq and dec_batch % SUBLANES == 0
    body = functools.partial(_mixa_body, n_prompt // ROW_TILE, dec_batch, dec_seq)
    return pl.pallas_call(
        body,
        grid=(n // ROW_TILE,),
        in_specs=[_row_spec(D_MODEL), _const_spec((1, D_MODEL)), _const_spec((D_MODEL, 2 * D_MODEL)),
                  _const_spec((1, D_MODEL)), _const_spec((1, D_MODEL)),
                  _const_spec((A_GROUPS, CHUNK, CHUNK)), _const_spec((CHUNK, D_MODEL)),
                  _const_spec((dec_seq, dec_seq, D_MODEL)), _const_spec((dec_seq, D_MODEL)),
                  _const_spec((D_MODEL, D_MODEL))],
        out_specs=[_row_spec(D_MODEL), pl.BlockSpec((ROW_TILE, D_MODEL), lambda i: (0, 0))],
        out_shape=[jax.ShapeDtypeStruct((n, D_MODEL), F32),
                   jax.ShapeDtypeStruct((ROW_TILE, D_MODEL), F32)],
        scratch_shapes=[pltpu.VMEM((ROW_TILE, D_MODEL), F32), pltpu.VMEM((ROW_TILE, D_MODEL), F32)],
        compiler_params=_params(),
        name="mixer_a",
    )(x, g, w_in, ln_g, ln_b, w_s, bexp, wrow, brow, w_out)


HIST_PAD = 16


def _mixc_body(n_prompt_tiles, tiles_per_seq, dec_batch, dec_seq,
               x_ref, g_ref, win_ref, wgrp_ref, scale_ref, wout_ref, hist_ref,
               o_ref, tail_ref, us_ref, ext_scr, pool_scr):
    i = pl.program_id(0)
    x = x_ref[...]
    h = _rms(x, g_ref[...]).astype(BF16)
    u = _dot(h, win_ref[...])

    @pl.when(i < n_prompt_tiles)
    def _():
        @pl.when(i % tiles_per_seq == 0)
        def _():
            ext_scr[0:HIST_PAD, :] = jnp.zeros((HIST_PAD, D_MODEL), F32)

        ext_scr[HIST_PAD:, :] = u
        pos = (i % tiles_per_seq) * ROW_TILE + lax.broadcasted_iota(jnp.int32, (ROW_TILE, 1), 0)
        for g, win in enumerate(POOL_WINDOWS):
            gs = slice(g * C_GROUP_DIM, (g + 1) * C_GROUP_DIM)
            tot = ext_scr[HIST_PAD:, gs]
            for d in range(1, win):
                tot = tot + ext_scr[HIST_PAD - d:HIST_PAD - d + ROW_TILE, gs]
            inv = 1.0 / jnp.minimum(pos + 1, win).astype(F32)
            pool_scr[:, gs] = tot * inv - ext_scr[HIST_PAD:, gs]
        tail = ext_scr[ROW_TILE:, :]
        tail_ref[0] = tail
        ext_scr[0:HIST_PAD, :] = tail

    @pl.when(i >= n_prompt_tiles)
    def _():
        ext_scr[0:ROW_TILE, :] = u
        us_ref[...] = u
        tail_ref[0] = u[ROW_TILE - HIST_PAD:, :]

        def ext_row(j, gs):
            if j < POOL_HIST:
                return hist_ref[j, :, gs]
            return ext_scr[(j - POOL_HIST) * dec_batch:(j - POOL_HIST + 1) * dec_batch, gs]

        for t in range(dec_seq):
            for g, win in enumerate(POOL_WINDOWS):
                gs = slice(g * C_GROUP_DIM, (g + 1) * C_GROUP_DIM)
                cur = ext_row(POOL_HIST + t, gs)
                tot = cur
                for d in range(1, win):
                    tot = tot + ext_row(POOL_HIST + t - d, gs)
                pool_scr[t * dec_batch:(t + 1) * dec_batch, gs] = tot * (1.0 / win) - cur

    parts = []
    for g in range(len(POOL_WINDOWS)):
        gs = slice(g * C_GROUP_DIM, (g + 1) * C_GROUP_DIM)
        parts.append(_dot(pool_scr[:, gs].astype(BF16), wgrp_ref[g]))
    mixed = jnp.concatenate(parts, axis=1) * scale_ref[...]
    o_ref[...] = x + _dot(mixed.astype(BF16), wout_ref[...])


def _mixer_c(x, g, w_in, w_grp, scale, w_out, hist_t, n_prompt, seq, dec_batch, dec_seq):
    n = x.shape[0]
    nt = n // ROW_TILE
    assert n - n_prompt == ROW_TILE == dec_batch * dec_seq and seq % ROW_TILE == 0
    assert POOL_HIST <= HIST_PAD and max(POOL_WINDOWS) - 1 <= POOL_HIST
    body = functools.partial(_mixc_body, n_prompt // ROW_TILE, seq // ROW_TILE, dec_batch, dec_seq)
    return pl.pallas_call(
        body,
        grid=(nt,),
        in_specs=[_row_spec(D_MODEL), _const_spec((1, D_MODEL)), _const_spec((D_MODEL, D_MODEL)),
                  _const_spec((len(POOL_WINDOWS), C_GROUP_DIM, C_GROUP_DIM)), _const_spec((1, D_MODEL)),
                  _const_spec((D_MODEL, D_MODEL)), _const_spec((POOL_HIST, dec_batch, D_MODEL))],
        out_specs=[_row_spec(D_MODEL),
                   pl.BlockSpec((1, HIST_PAD, D_MODEL), lambda i: (i, 0, 0)),
                   pl.BlockSpec((ROW_TILE, D_MODEL), lambda i: (0, 0))],
        out_shape=[jax.ShapeDtypeStruct((n, D_MODEL), F32),
                   jax.ShapeDtypeStruct((nt, HIST_PAD, D_MODEL), F32),
                   jax.ShapeDtypeStruct((ROW_TILE, D_MODEL), F32)],
        scratch_shapes=[pltpu.VMEM((HIST_PAD + ROW_TILE, D_MODEL), F32),
                        pltpu.VMEM((ROW_TILE, D_MODEL), F32)],
        compiler_params=_params(),
        name="mixer_c",
    )(x, g, w_in, w_grp, scale, w_out, hist_t)


def _rope(xh, cos_t, sin_t, lane):
    partner = jnp.where(lane < ROT_DIM // 2, pltpu.roll(xh, LANES - ROT_DIM // 2, 1),
                        pltpu.roll(xh, ROT_DIM // 2, 1))
    return xh * cos_t + partner * sin_t


def _qkv_body(n_prompt_tiles, x_ref, g_ref, w_ref, cos_ref, sin_ref,
              q_ref, kp_ref, vp_ref, ks_ref, vs_ref):
    i = pl.program_id(0)
    x = x_ref[...]
    h = _rms(x, g_ref[...]).astype(BF16)
    qkv = _dot(h, w_ref[...])
    cos_t = cos_ref[...]
    sin_t = sin_ref[...]
    lane = lax.broadcasted_iota(jnp.int32, (ROW_TILE, LANES), 1)
    q_parts, k_parts = [], []
    for hd in range(N_HEADS):
        q_parts.append(_rope(qkv[:, hd * HEAD_DIM:(hd + 1) * HEAD_DIM], cos_t, sin_t, lane))
        off = D_MODEL + hd * HEAD_DIM
        k_parts.append(_rope(qkv[:, off:off + HEAD_DIM], cos_t, sin_t, lane))
    q_ref[...] = jnp.concatenate(q_parts, axis=1)
    k = jnp.concatenate(k_parts, axis=1)
    v = qkv[:, 2 * D_MODEL:]

    @pl.when(i < n_prompt_tiles)
    def _():
        kp_ref[...] = k
        vp_ref[...] = v

    @pl.when(i >= n_prompt_tiles)
    def _():
        ks_ref[...] = k
        vs_ref[...] = v


def _qkv(x, g, w_qkv, cos_t, sin_t, n_prompt):
    n = x.shape[0]
    npt = n_prompt // ROW_TILE
    assert n - n_prompt == ROW_TILE
    prompt_spec = pl.BlockSpec((ROW_TILE, D_MODEL), lambda i: (jnp.minimum(i, npt - 1), 0))
    sample_spec = pl.BlockSpec((ROW_TILE, D_MODEL), lambda i: (0, 0))
    return pl.pallas_call(
        functools.partial(_qkv_body, npt),
        grid=(n // ROW_TILE,),
        in_specs=[_row_spec(D_MODEL), _const_spec((1, D_MODEL)), _const_spec((D_MODEL, 3 * D_MODEL)),
                  _row_spec(LANES), _row_spec(LANES)],
        out_specs=[_row_spec(D_MODEL), prompt_spec, prompt_spec, sample_spec, sample_spec],
        out_shape=[jax.ShapeDtypeStruct((n, D_MODEL), F32),
                   jax.ShapeDtypeStruct((n_prompt, D_MODEL), F32),
                   jax.ShapeDtypeStruct((n_prompt, D_MODEL), F32),
                   jax.ShapeDtypeStruct((ROW_TILE, D_MODEL), F32),
                   jax.ShapeDtypeStruct((ROW_TILE, D_MODEL), F32)],
        compiler_params=_params(),
        name="moba_qkv",
    )(x, g, w_qkv, cos_t, sin_t)


def _top_k_mask(gate, valid, k, axis):
    idx = lax.broadcasted_iota(jnp.int32, gate.shape, axis)
    size = gate.shape[axis]
    g = jnp.where(valid, gate, -jnp.inf)
    chosen = jnp.zeros(gate.shape, dtype=jnp.bool_)
    for _ in range(k):
        mx = jnp.max(g, axis=axis, keepdims=True)
        first = jnp.min(jnp.where(g == mx, idx, size), axis=axis, keepdims=True)
        pick = (idx == first) & (mx > -jnp.inf)
        chosen = chosen | pick
        g = jnp.where(pick, -jnp.inf, g)
    return chosen


def _pattn_body(n_blocks, q_ref, k_ref, v_ref, o_ref, kaug_scr, v_scr, kmean_scr):
    qb = pl.program_id(2)
    scale = HEAD_DIM ** -0.5

    @pl.when(qb == 0)
    def _():
        kmean_scr[...] = jnp.zeros(kmean_scr.shape, F32)
        lane = lax.broadcasted_iota(jnp.int32, (MOBA_BLOCK, LANES), 1)

        def fill(j, carry):
            rows = pl.ds(pl.multiple_of(j * MOBA_BLOCK, MOBA_BLOCK), MOBA_BLOCK)
            kj = k_ref[rows, :]
            kaug_scr[rows, 0:HEAD_DIM] = kj.astype(BF16)
            kaug_scr[rows, HEAD_DIM:] = jnp.where(lane == j, 1.0, 0.0).astype(BF16)
            v_scr[rows, :] = v_ref[rows, :].astype(BF16)
            kmean_scr[pl.ds(j, 1), :] = jnp.mean(kj, axis=0, keepdims=True)
            return carry

        lax.fori_loop(0, n_blocks, fill, 0)

    q = q_ref[...]
    q_hi, q_lo = _split_bf16(q)
    m_hi, m_lo = _split_bf16(kmean_scr[...])
    gate = _dot_t1(q_hi, m_hi) + _dot_t1(q_hi, m_lo) + _dot_t1(q_lo, m_hi)
    blk = lax.broadcasted_iota(jnp.int32, gate.shape, 1)
    allowed = _top_k_mask(gate, blk < qb, MOBA_TOP_K, 1)
    bias = jnp.where(allowed, 0.0, NEG_BIG).astype(BF16)
    q_aug = jnp.concatenate([q_hi, bias], axis=1)

    own = pl.ds(pl.multiple_of(qb * MOBA_BLOCK, MOBA_BLOCK), MOBA_BLOCK)
    s = _dot_t1(q_hi, kaug_scr[own, 0:HEAD_DIM]) * scale
    r = lax.broadcasted_iota(jnp.int32, s.shape, 0)
    c = lax.broadcasted_iota(jnp.int32, s.shape, 1)
    s = jnp.where(c <= r, s, NEG_BIG)
    m0 = jnp.max(s, axis=1, keepdims=True)
    p = jnp.exp(s - m0)
    l0 = jnp.sum(p, axis=1, keepdims=True)
    acc0 = _dot(p.astype(BF16), v_scr[own, :])

    def step(j, carry):
        m, l, acc = carry
        rows = pl.ds(pl.multiple_of(j * MOBA_BLOCK, MOBA_BLOCK), MOBA_BLOCK)
        sj = _dot_t1(q_aug, kaug_scr[rows, :]) * scale
        m_new = jnp.maximum(m, jnp.max(sj, axis=1, keepdims=True))
        alpha = jnp.exp(m - m_new)
        pj = jnp.exp(sj - m_new)
        l = alpha * l + jnp.sum(pj, axis=1, keepdims=True)
        acc = alpha * acc + _dot(pj.astype(BF16), v_scr[rows, :])
        return m_new, l, acc

    m, l, acc = lax.fori_loop(0, qb, step, (m0, l0, acc0))
    o_ref[...] = (acc / l).astype(o_ref.dtype)


def _prompt_attention(q_all, k_p, v_p, batch, seq):
    n_blocks = seq // MOBA_BLOCK
    assert seq % MOBA_BLOCK == 0 and n_blocks <= LANES
    qspec = pl.BlockSpec((MOBA_BLOCK, HEAD_DIM), lambda b, h, i: (b * n_blocks + i, h))
    kvspec = pl.BlockSpec((seq, HEAD_DIM), lambda b, h, i: (b, h))
    return pl.pallas_call(
        functools.partial(_pattn_body, n_blocks),
        grid=(batch, N_HEADS, n_blocks),
        in_specs=[qspec, kvspec, kvspec],
        out_specs=qspec,
        out_shape=jax.ShapeDtypeStruct((batch * seq, D_MODEL), BF16),
        scratch_shapes=[pltpu.VMEM((seq, 2 * LANES), BF16), pltpu.VMEM((seq, HEAD_DIM), BF16),
                        pltpu.VMEM((LANES, HEAD_DIM), F32)],
        compiler_params=_params(3),
        name="moba_prompt_attn",
    )(q_all, k_p, v_p)


def _sattn_body(n_pages, dec_seq, pt_ref, q_ref, kn_ref, vn_ref, ck_hbm, cv_hbm, o_ref,
                kbuf, vbuf, sem, s_scr, kmean_scr):
    n = pl.program_id(0)
    n_seq = pl.num_programs(0)
    slot = n % 2
    scale = HEAD_DIM ** -0.5
    pages_per_block = MOBA_BLOCK // PAGE_SIZE
    n_blocks = n_pages // pages_per_block

    def page_copy(hbm, buf, seq, sl, p, kind):
        return pltpu.make_async_copy(hbm.at[pt_ref[seq * n_pages + p]], buf.at[sl, p], sem.at[kind, sl])

    def start_all(seq, sl):
        for p in range(n_pages):
            page_copy(ck_hbm, kbuf, seq, sl, p, 0).start()
            page_copy(cv_hbm, vbuf, seq, sl, p, 1).start()

    @pl.when(n == 0)
    def _():
        start_all(0, 0)

    @pl.when(n + 1 < n_seq)
    def _():
        start_all(n + 1, 1 - slot)

    for p in range(n_pages):
        page_copy(ck_hbm, kbuf, n, slot, p, 0).wait()
        page_copy(cv_hbm, vbuf, n, slot, p, 1).wait()

    pad_rows = jnp.zeros((PAGE_SIZE - dec_seq, D_MODEL), F32)
    q_pad = jnp.concatenate([q_ref[0], pad_rows], axis=0)
    row_t = lax.broadcasted_iota(jnp.int32, (PAGE_SIZE, LANES), 0)
    col_c = lax.broadcasted_iota(jnp.int32, (PAGE_SIZE, LANES), 1)
    pick = jnp.where((col_c % dec_seq == row_t) & (col_c < N_HEADS * dec_seq), 1.0, 0.0).astype(BF16)
    head_r = lax.broadcasted_iota(jnp.int32, (D_MODEL, LANES), 0) // HEAD_DIM
    head_c = lax.broadcasted_iota(jnp.int32, (D_MODEL, LANES), 1) // dec_seq
    same_head = head_r == head_c
    q_hi, q_lo = _split_bf16(q_pad)
    qbd_hi = jnp.where(same_head, _dot_t0(q_hi, pick), 0.0).astype(BF16)
    qbd_lo = jnp.where(same_head, _dot_t0(q_lo, pick), 0.0).astype(BF16)

    kmean_scr[...] = jnp.zeros(kmean_scr.shape, F32)
    for b in range(n_blocks):
        ksum = None
        for r in range(pages_per_block):
            p = b * pages_per_block + r
            kp = kbuf[slot, p]
            s_scr[p * PAGE_SIZE:(p + 1) * PAGE_SIZE, :] = _dot(kp.astype(BF16), qbd_hi) * scale
            part = jnp.sum(kp, axis=0, keepdims=True)
            ksum = part if ksum is None else ksum + part
        kmean_scr[b:b + 1, :] = ksum * (1.0 / MOBA_BLOCK)
    m_hi, m_lo = _split_bf16(kmean_scr[...])
    gate = _dot(m_hi, qbd_hi) + _dot(m_lo, qbd_hi) + _dot(m_hi, qbd_lo)
    blk = lax.broadcasted_iota(jnp.int32, gate.shape, 0)
    allowed = _top_k_mask(gate, blk < n_blocks, min(MOBA_TOP_K, n_blocks), 0)
    bias = jnp.where(allowed, 0.0, NEG_BIG)

    k_pad = jnp.concatenate([kn_ref[0], pad_rows], axis=0)
    v_pad = jnp.concatenate([vn_ref[0], pad_rows], axis=0)
    s_new = _dot(k_pad.astype(BF16), qbd_hi) * scale
    s_new = jnp.where((row_t <= col_c % dec_seq) & (row_t < dec_seq), s_new, NEG_BIG)

    m = jnp.max(s_new, axis=0, keepdims=True)
    for b in range(n_blocks):
        rows = slice(b * MOBA_BLOCK, (b + 1) * MOBA_BLOCK)
        sb = s_scr[rows, :] + bias[b:b + 1, :]
        s_scr[rows, :] = sb
        m = jnp.maximum(m, jnp.max(sb, axis=0, keepdims=True))
    p_new = jnp.exp(s_new - m)
    l = jnp.sum(p_new, axis=0, keepdims=True)
    for p in range(n_pages):
        rows = slice(p * PAGE_SIZE, (p + 1) * PAGE_SIZE)
        pp = jnp.exp(s_scr[rows, :] - m)
        s_scr[rows, :] = pp
        l = l + jnp.sum(pp, axis=0, keepdims=True)
    inv = 1.0 / l
    out = _dot_t0((p_new * inv).astype(BF16), v_pad.astype(BF16))
    for p in range(n_pages):
        rows = slice(p * PAGE_SIZE, (p + 1) * PAGE_SIZE)
        out = out + _dot_t0((s_scr[rows, :] * inv).astype(BF16), vbuf[slot, p].astype(BF16))
    out_c = lax.broadcasted_iota(jnp.int32, out.shape, 0)
    out_h = lax.broadcasted_iota(jnp.int32, out.shape, 1) // HEAD_DIM
    for t in range(dec_seq):
        keep = (out_c % dec_seq == t) & (out_c // dec_seq == out_h)
        o_ref[0, t:t + 1, :] = jnp.sum(jnp.where(keep, out, 0.0), axis=0, keepdims=True)


def _sample_attention(page_table, q_s, k_s, v_s, cache_k, cache_v):
    n_seq, n_pages = page_table.shape
    dec_seq = q_s.shape[1]
    assert n_pages % (MOBA_BLOCK // PAGE_SIZE) == 0 and N_HEADS * dec_seq <= LANES and dec_seq <= SUBLANES
    row_spec = pl.BlockSpec((1, dec_seq, D_MODEL), lambda n, pt: (n, 0, 0))
    any_spec = pl.BlockSpec(memory_space=pl.ANY)
    grid_spec = pltpu.PrefetchScalarGridSpec(
        num_scalar_prefetch=1,
        grid=(n_seq,),
        in_specs=[row_spec, row_spec, row_spec, any_spec, any_spec],
        out_specs=row_spec,
        scratch_shapes=[pltpu.VMEM((2, n_pages, PAGE_SIZE, D_MODEL), F32),
                        pltpu.VMEM((2, n_pages, PAGE_SIZE, D_MODEL), F32),
                        pltpu.SemaphoreType.DMA((2, 2)),
                        pltpu.VMEM((n_pages * PAGE_SIZE, LANES), F32),
                        pltpu.VMEM((2 * SUBLANES, D_MODEL), F32)],
    )
    return pl.pallas_call(
        functools.partial(_sattn_body, n_pages, dec_seq),
        grid_spec=grid_spec,
        out_shape=jax.ShapeDtypeStruct((n_seq, dec_seq, D_MODEL), F32),
        compiler_params=_params(),
        name="moba_sample_attn",
    )(page_table.reshape(-1), q_s, k_s, v_s, cache_k, cache_v)


def _oproj_body(n_prompt_tiles, x_ref, op_ref, os_ref, w_ref, o_ref):
    i = pl.program_id(0)
    o = jnp.where(i < n_prompt_tiles, op_ref[...], os_ref[...])
    o_ref[...] = x_ref[...] + _dot(o, w_ref[...])


def _out_proj(x, o_p, o_s, w_out, n_prompt):
    n = x.shape[0]
    npt = n_prompt // ROW_TILE
    return pl.pallas_call(
        functools.partial(_oproj_body, npt),
        grid=(n // ROW_TILE,),
        in_specs=[_row_spec(D_MODEL),
                  pl.BlockSpec((ROW_TILE, D_MODEL), lambda i: (jnp.minimum(i, npt - 1), 0)),
                  pl.BlockSpec((ROW_TILE, D_MODEL), lambda i: (0, 0)),
                  _const_spec((D_MODEL, D_MODEL))],
        out_specs=_row_spec(D_MODEL),
        out_shape=jax.ShapeDtypeStruct((n, D_MODEL), F32),
        compiler_params=_params(),
        name="moba_out_proj",
    )(x, o_p, o_s, w_out)


def _rope_tables(seq, n_prompt, past_len, dec_batch, dec_seq):
    half = ROT_DIM // 2
    pos = jnp.concatenate([jnp.arange(n_prompt, dtype=jnp.int32) % seq,
                           past_len + jnp.repeat(jnp.arange(dec_seq, dtype=jnp.int32), dec_batch)])
    inv_freq = ROPE_THETA ** (-jnp.arange(half, dtype=F32) / half)
    ang = pos.astype(F32)[:, None] * inv_freq[None, :]
    cos, sin = jnp.cos(ang), jnp.sin(ang)
    ones = jnp.ones((pos.shape[0], LANES - ROT_DIM), F32)
    cos_t = jnp.concatenate([cos, cos, ones], axis=1)
    sin_t = jnp.concatenate([-sin, sin, 0.0 * ones], axis=1)
    return cos_t, sin_t


def _to_time_major(a):
    return jnp.swapaxes(a, 0, 1).reshape((a.shape[0] * a.shape[1],) + a.shape[2:])


def _from_time_major(a, dec_batch, dec_seq):
    return jnp.swapaxes(a.reshape((dec_seq, dec_batch) + a.shape[1:]), 0, 1)


def kernel(x_prompt, x_sample, cache_k, cache_v, state_pool, page_table, p_prompt, p_sample, norm_ffn1, norm_mix, norm_ffn2, norm_ple, norm_final, ffn1_w_gate, ffn1_w_up, ffn1_w_down, ffn2_w_gate, ffn2_w_up, ffn2_w_down, ple_w_gate, ple_w_proj, a_w_in, a_ln_g, a_ln_b, a_w_s, a_b_s, a_w_out, b_w_qkv, b_w_out, c_w_in, c_w_grp, c_scale, c_w_out):
    batch, seq, d = x_prompt.shape
    dec_batch, dec_seq, _ = x_sample.shape
    n_prompt = batch * seq
    n_sample = dec_batch * dec_seq
    n_pages = page_table.shape[1]
    past_len = n_pages * PAGE_SIZE
    assert d == D_MODEL and n_sample == ROW_TILE and n_prompt % ROW_TILE == 0
    assert past_len % MOBA_BLOCK == 0

    x = jnp.concatenate([x_prompt.reshape(n_prompt, d), _to_time_major(x_sample)], axis=0)
    p_all = jnp.concatenate([p_prompt.reshape(DEPTH, n_prompt, PLE_DIM),
                             jnp.swapaxes(p_sample, 1, 2).reshape(DEPTH, n_sample, PLE_DIM)], axis=1)
    row = lambda v: v.reshape(1, -1)
    bf = lambda w: w.astype(BF16)

    k_p, v_p, k_s, v_s, pool_p, pool_s, chunk_v = [], [], [], [], [], [], []
    y_prompt = y_sample = None
    for i in range(DEPTH):
        kind, j = LAYER_MIXERS[i], LAYER_SLOT[i]
        x = _ffn(x, row(norm_ffn1[i]), bf(ffn1_w_gate[i]), bf(ffn1_w_up[i]), bf(ffn1_w_down[i]))
        g_mix = row(norm_mix[i])
        if kind == 0:
            bexp = jnp.repeat(a_b_s[j].T, LANES, axis=1)
            wrow = jnp.repeat(jnp.transpose(a_w_s[j][:, :dec_seq, :dec_seq], (1, 2, 0)), LANES, axis=2)
            brow = jnp.repeat(a_b_s[j][:, :dec_seq].T, LANES, axis=1)
            x, vs = _mixer_a(x, g_mix, bf(a_w_in[j]), row(a_ln_g[j]), row(a_ln_b[j]), a_w_s[j], bexp,
                             wrow, brow, bf(a_w_out[j]), n_prompt, dec_batch, dec_seq)
            chunk_v.append(_from_time_major(vs, dec_batch, dec_seq))
        elif kind == 1:
            cos_t, sin_t = _rope_tables(seq, n_prompt, past_len, dec_batch, dec_seq)
            q_all, kp, vp, ks, vs = _qkv(x, g_mix, bf(b_w_qkv[j]), cos_t, sin_t, n_prompt)
            o_p = _prompt_attention(q_all, kp, vp, batch, seq)
            q_s = _from_time_major(q_all[n_prompt:], dec_batch, dec_seq)
            ks_n = _from_time_major(ks, dec_batch, dec_seq)
            vs_n = _from_time_major(vs, dec_batch, dec_seq)
            o_s = _sample_attention(page_table, q_s, ks_n, vs_n,
                                    cache_k[j].reshape(-1, PAGE_SIZE, D_MODEL),
                                    cache_v[j].reshape(-1, PAGE_SIZE, D_MODEL))
            o_s = _to_time_major(o_s).astype(BF16)
            x = _out_proj(x, o_p, o_s, bf(b_w_out[j]), n_prompt)
            k_p.append(kp.reshape(batch, seq, N_HEADS, HEAD_DIM))
            v_p.append(vp.reshape(batch, seq, N_HEADS, HEAD_DIM))
            k_s.append(ks_n.reshape(dec_batch, dec_seq, N_HEADS, HEAD_DIM))
            v_s.append(vs_n.reshape(dec_batch, dec_seq, N_HEADS, HEAD_DIM))
        else:
            hist_t = jnp.swapaxes(state_pool[j], 0, 1)
            x, tails, us = _mixer_c(x, g_mix, bf(c_w_in[j]), bf(c_w_grp[j]), row(c_scale[j]), bf(c_w_out[j]),
                                    hist_t, n_prompt, seq, dec_batch, dec_seq)
            tps = seq // ROW_TILE
            pool_p.append(tails[tps - 1:batch * tps:tps, HIST_PAD - POOL_HIST:, :])
            u_new = _from_time_major(us, dec_batch, dec_seq)
            pool_s.append(jnp.concatenate([state_pool[j], u_new], axis=1)[:, -POOL_HIST:])
        x = _ffn(x, row(norm_ffn2[i]), bf(ffn2_w_gate[i]), bf(ffn2_w_up[i]), bf(ffn2_w_down[i]))
        if i + 1 < DEPTH:
            x = _ple(x, row(norm_ple[i]), bf(ple_w_gate[i]), p_all[i], bf(ple_w_proj[i]))
        else:
            y_prompt, y_sample = _ple_final(x, row(norm_ple[i]), bf(ple_w_gate[i]), p_all[i],
                                            bf(ple_w_proj[i]), row(norm_final), n_prompt)
    y_prompt = y_prompt.reshape(batch, seq, d)
    y_sample = _from_time_major(y_sample, dec_batch, dec_seq)
    return (y_prompt, y_sample, jnp.stack(k_p), jnp.stack(v_p), jnp.stack(k_s), jnp.stack(v_s),
            jnp.stack(pool_p), jnp.stack(pool_s), jnp.stack(chunk_v))
```

```python
import functools
import math

import jax
import jax.numpy as jnp
from jax import lax
from jax.experimental import pallas as pl
from jax.experimental.pallas import tpu as pltpu

F32 = jnp.float32
BF16 = jnp.bfloat16

D_MODEL = 1024
DEPTH = 4
D_FF = 2816
PLE_DIM = 256
RMS_EPS = 1e-6
LN_EPS = 1e-5
CHUNK = 128
A_GROUPS = 8
N_HEADS = 8
HEAD_DIM = 128
ROT_DIM = 32
ROPE_THETA = 500000.0
MOBA_BLOCK = 256
MOBA_TOP_K = 3
PAGE_SIZE = 128
POOL_WINDOWS = (2, 4, 8, 16)
C_GROUP_DIM = 256
POOL_HIST = 15
LAYER_MIXERS = (0, 1, 2, 0)
LAYER_SLOT = (0, 0, 0, 1)

LANES = 128
SUBLANES = 8
VMEM_LIMIT = 56 * 1024 * 1024

ROW_TILE = 512
NEG_BIG = -1e30


def _dot(a, b):
    return jnp.dot(a, b, preferred_element_type=F32)


def _dot_t0(a, b):
    return lax.dot_general(a, b, (((0,), (0,)), ((), ())), preferred_element_type=F32)


def _dot_t1(a, b):
    return lax.dot_general(a, b, (((1,), (1,)), ((), ())), preferred_element_type=F32)


def _rms(x, g):
    return x * lax.rsqrt(jnp.mean(x * x, axis=-1, keepdims=True) + RMS_EPS) * g


def _split_bf16(x):
    hi = x.astype(BF16)
    lo = (x - hi.astype(F32)).astype(BF16)
    return hi, lo


def _const_spec(shape):
    nd = len(shape)
    return pl.BlockSpec(shape, lambda *_: (0,) * nd, pipeline_mode=pl.Buffered(1))


def _row_spec(width):
    return pl.BlockSpec((ROW_TILE, width), lambda i: (i, 0))


def _params(n_axes=1):
    return pltpu.CompilerParams(dimension_semantics=("arbitrary",) * n_axes,
                                vmem_limit_bytes=VMEM_LIMIT)


def _ffn_body(n_prompt_tiles, split_x, ple, final, *refs):
    refs = list(refs)
    is_prompt = pl.program_id(0) < n_prompt_tiles
    if split_x:
        xp_ref, xs_ref = refs[:2]
        refs = refs[2:]
        x = jnp.where(is_prompt, xp_ref[...], xs_ref[...])
    else:
        x = refs.pop(0)[...]
    g_ref, wg_ref, wu_ref, wd_ref = refs[:4]
    refs = refs[4:]
    h = _rms(x, g_ref[...]).astype(BF16)
    gate = _dot(h, wg_ref[...])
    up = _dot(h, wu_ref[...])
    a = (gate * jax.nn.sigmoid(gate) * up).astype(BF16)
    x = x + 0.5 * _dot(a, wd_ref[...])
    if ple:
        gp_ref, wgate_ref, pp_ref, ps_ref, wproj_ref = refs[:5]
        refs = refs[5:]
        p = jnp.where(is_prompt, pp_ref[...], ps_ref[...]).astype(BF16)
        h = _rms(x, gp_ref[...]).astype(BF16)
        x = x + jax.nn.sigmoid(_dot(h, wgate_ref[...])) * _dot(p, wproj_ref[...])
    if not final:
        (o_ref,) = refs
        o_ref[...] = x
        return
    gf_ref, yp_ref, ys_ref = refs
    y = _rms(x, gf_ref[...])

    @pl.when(is_prompt)
    def _():
        yp_ref[...] = y

    @pl.when(jnp.logical_not(is_prompt))
    def _():
        ys_ref[...] = y


def _ffn(x, g, wg, wu, wd, n_prompt, ple=None, final_gain=None):
    npt = n_prompt // ROW_TILE
    prompt_rows = lambda i: (jnp.minimum(i, npt - 1), 0)
    sample_rows = lambda i: (0, 0)
    split_x = isinstance(x, tuple)
    if split_x:
        args = list(x)
        in_specs = [pl.BlockSpec((ROW_TILE, D_MODEL), prompt_rows), pl.BlockSpec((ROW_TILE, D_MODEL), sample_rows)]
        n = n_prompt + x[1].shape[0]
    else:
        args = [x]
        in_specs = [_row_spec(D_MODEL)]
        n = x.shape[0]
    assert n - n_prompt == ROW_TILE
    args += [g, wg, wu, wd]
    in_specs += [_const_spec((1, D_MODEL)), _const_spec((D_MODEL, D_FF)), _const_spec((D_MODEL, D_FF)),
                 _const_spec((D_FF, D_MODEL))]
    if ple is not None:
        gp, wgate, p_prompt, p_sample, wproj, layer = ple
        args += [gp, wgate, p_prompt, p_sample, wproj]
        in_specs += [_const_spec((1, D_MODEL)), _const_spec((D_MODEL, D_MODEL)),
                     pl.BlockSpec((None, ROW_TILE, PLE_DIM), lambda i: (layer, jnp.minimum(i, npt - 1), 0)),
                     pl.BlockSpec((None, ROW_TILE, PLE_DIM), lambda i: (layer, 0, 0)),
                     _const_spec((PLE_DIM, D_MODEL))]
    if final_gain is None:
        out_specs = _row_spec(D_MODEL)
        out_shape = jax.ShapeDtypeStruct((n, D_MODEL), F32)
    else:
        args.append(final_gain)
        in_specs.append(_const_spec((1, D_MODEL)))
        out_specs = [pl.BlockSpec((ROW_TILE, D_MODEL), prompt_rows), pl.BlockSpec((ROW_TILE, D_MODEL), sample_rows)]
        out_shape = [jax.ShapeDtypeStruct((n_prompt, D_MODEL), F32), jax.ShapeDtypeStruct((ROW_TILE, D_MODEL), F32)]
    return pl.pallas_call(
        functools.partial(_ffn_body, npt, split_x, ple is not None, final_gain is not None),
        grid=(n // ROW_TILE,),
        in_specs=in_specs,
        out_specs=out_specs,
        out_shape=out_shape,
        compiler_params=_params(),
        name="ffn_ple" if ple is not None else "ffn",
    )(*args)


def _mixa_body(n_prompt_tiles, dec_batch, dec_seq,
               x_ref, g_ref, win_ref, lng_ref, lnb_ref, ws_ref, bexp_ref, wrow_ref, brow_ref, wout_ref,
               o_ref, vs_ref, v_scr, m_scr):
    i = pl.program_id(0)
    x = x_ref[...]
    h = _rms(x, g_ref[...]).astype(BF16)
    z = _dot(h, win_ref[...])
    z = 0.5 * z * (1.0 + lax.erf(z * (1.0 / math.sqrt(2.0))))
    u = z[:, :D_MODEL]
    v = z[:, D_MODEL:]
    mu = jnp.mean(v, axis=-1, keepdims=True)
    vc = v - mu
    var = jnp.mean(vc * vc, axis=-1, keepdims=True)
    v_scr[...] = vc * lax.rsqrt(var + LN_EPS) * lng_ref[...] + lnb_ref[...]
    n_chunks = ROW_TILE // CHUNK

    @pl.when(i < n_prompt_tiles)
    def _():
        rows = lax.broadcasted_iota(jnp.int32, (CHUNK, CHUNK), 0)
        cols = lax.broadcasted_iota(jnp.int32, (CHUNK, CHUNK), 1)
        causal = cols <= rows
        for g in range(A_GROUPS):
            gs = slice(g * LANES, (g + 1) * LANES)
            w = jnp.where(causal, ws_ref[g], 0.0).astype(BF16)
            rhs = jnp.concatenate([v_scr[c * CHUNK:(c + 1) * CHUNK, gs] for c in range(n_chunks)],
                                  axis=1).astype(BF16)
            mg = _dot(w, rhs)
            for c in range(n_chunks):
                m_scr[c * CHUNK:(c + 1) * CHUNK, gs] = mg[:, c * LANES:(c + 1) * LANES] + bexp_ref[:, gs]

    @pl.when(i >= n_prompt_tiles)
    def _():
        for t in range(dec_seq):
            acc = brow_ref[t:t + 1, :] + wrow_ref[t, 0:1, :] * v_scr[0:dec_batch, :]
            for s in range(1, t + 1):
                acc = acc + wrow_ref[t, s:s + 1, :] * v_scr[s * dec_batch:(s + 1) * dec_batch, :]
            m_scr[t * dec_batch:(t + 1) * dec_batch, :] = acc
        vs_ref[...] = v_scr[...]

    y = _dot((u * m_scr[...]).astype(BF16), wout_ref[...])
    o_ref[...] = x + y


def _mixer_a(x, g, w_in, ln_g, ln_b, w_s, bexp, wrow, brow, w_out, n_prompt, dec_batch, dec_seq):
    n = x.shape[0]
    assert n - n_prompt == ROW_TILE == dec_batch * dec_seq and dec_batch % SUBLANES == 0
    body = functools.partial(_mixa_body, n_prompt // ROW_TILE, dec_batch, dec_seq)
    return pl.pallas_call(
        body,
        grid=(n // ROW_TILE,),
        in_specs=[_row_spec(D_MODEL), _const_spec((1, D_MODEL)), _const_spec((D_MODEL, 2 * D_MODEL)),
                  _const_spec((1, D_MODEL)), _const_spec((1, D_MODEL)),
                  _const_spec((A_GROUPS, CHUNK, CHUNK)), _const_spec((CHUNK, D_MODEL)),
                  _const_spec((dec_seq, dec_seq, D_MODEL)), _const_spec((dec_seq, D_MODEL)),
                  _const_spec((D_MODEL, D_MODEL))],
        out_specs=[_row_spec(D_MODEL), pl.BlockSpec((ROW_TILE, D_MODEL), lambda i: (0, 0))],
        out_shape=[jax.ShapeDtypeStruct((n, D_MODEL), F32),
                   jax.ShapeDtypeStruct((ROW_TILE, D_MODEL), F32)],
        scratch_shapes=[pltpu.VMEM((ROW_TILE, D_MODEL), F32), pltpu.VMEM((ROW_TILE, D_MODEL), F32)],
        compiler_params=_params(),
        name="mixer_a",
    )(x, g, w_in, ln_g, ln_b, w_s, bexp, wrow, brow, w_out)


HIST_PAD = 16


def _mixc_body(n_prompt_tiles, tiles_per_seq, dec_batch, dec_seq,
               x_ref, g_ref, win_ref, wgrp_ref, scale_ref, wout_ref, hist_ref,
               o_ref, tail_ref, us_ref, ext_scr, pool_scr):
    i = pl.program_id(0)
    x = x_ref[...]
    h = _rms(x, g_ref[...]).astype(BF16)
    u = _dot(h, win_ref[...])

    @pl.when(i < n_prompt_tiles)
    def _():
        @pl.when(i % tiles_per_seq == 0)
        def _():
            ext_scr[0:HIST_PAD, :] = jnp.zeros((HIST_PAD, D_MODEL), F32)

        ext_scr[HIST_PAD:, :] = u
        pos = (i % tiles_per_seq) * ROW_TILE + lax.broadcasted_iota(jnp.int32, (ROW_TILE, 1), 0)
        for g, win in enumerate(POOL_WINDOWS):
            gs = slice(g * C_GROUP_DIM, (g + 1) * C_GROUP_DIM)
            tot = ext_scr[HIST_PAD:, gs]
            for d in range(1, win):
                tot = tot + ext_scr[HIST_PAD - d:HIST_PAD - d + ROW_TILE, gs]
            inv = 1.0 / jnp.minimum(pos + 1, win).astype(F32)
            pool_scr[:, gs] = tot * inv - ext_scr[HIST_PAD:, gs]
        tail = ext_scr[ROW_TILE:, :]
        tail_ref[0] = tail
        ext_scr[0:HIST_PAD, :] = tail

    @pl.when(i >= n_prompt_tiles)
    def _():
        ext_scr[0:ROW_TILE, :] = u
        us_ref[...] = u
        tail_ref[0] = u[ROW_TILE - HIST_PAD:, :]

        def ext_row(j, gs):
            if j < POOL_HIST:
                return hist_ref[j, :, gs]
            return ext_scr[(j - POOL_HIST) * dec_batch:(j - POOL_HIST + 1) * dec_batch, gs]

        for t in range(dec_seq):
            for g, win in enumerate(POOL_WINDOWS):
                gs = slice(g * C_GROUP_DIM, (g + 1) * C_GROUP_DIM)
                cur = ext_row(POOL_HIST + t, gs)
                tot = cur
                for d in range(1, win):
                    tot = tot + ext_row(POOL_HIST + t - d, gs)
                pool_scr[t * dec_batch:(t + 1) * dec_batch, gs] = tot * (1.0 / win) - cur

    parts = []
    for g in range(len(POOL_WINDOWS)):
        gs = slice(g * C_GROUP_DIM, (g + 1) * C_GROUP_DIM)
        parts.append(_dot(pool_scr[:, gs].astype(BF16), wgrp_ref[g]))
    mixed = jnp.concatenate(parts, axis=1) * scale_ref[...]
    o_ref[...] = x + _dot(mixed.astype(BF16), wout_ref[...])


def _mixer_c(x, g, w_in, w_grp, scale, w_out, hist_t, n_prompt, seq, dec_batch, dec_seq):
    n = x.shape[0]
    nt = n // ROW_TILE
    assert n - n_prompt == ROW_TILE == dec_batch * dec_seq and seq % ROW_TILE == 0
    assert POOL_HIST <= HIST_PAD and max(POOL_WINDOWS) - 1 <= POOL_HIST
    body = functools.partial(_mixc_body, n_prompt // ROW_TILE, seq // ROW_TILE, dec_batch, dec_seq)
    return pl.pallas_call(
        body,
        grid=(nt,),
        in_specs=[_row_spec(D_MODEL), _const_spec((1, D_MODEL)), _const_spec((D_MODEL, D_MODEL)),
                  _const_spec((len(POOL_WINDOWS), C_GROUP_DIM, C_GROUP_DIM)), _const_spec((1, D_MODEL)),
                  _const_spec((D_MODEL, D_MODEL)), _const_spec((POOL_HIST, dec_batch, D_MODEL))],
        out_specs=[_row_spec(D_MODEL),
                   pl.BlockSpec((1, HIST_PAD, D_MODEL), lambda i: (i, 0, 0)),
                   pl.BlockSpec((ROW_TILE, D_MODEL), lambda i: (0, 0))],
        out_shape=[jax.ShapeDtypeStruct((n, D_MODEL), F32),
                   jax.ShapeDtypeStruct((nt, HIST_PAD, D_MODEL), F32),
                   jax.ShapeDtypeStruct((ROW_TILE, D_MODEL), F32)],
        scratch_shapes=[pltpu.VMEM((HIST_PAD + ROW_TILE, D_MODEL), F32),
                        pltpu.VMEM((ROW_TILE, D_MODEL), F32)],
        compiler_params=_params(),
        name="mixer_c",
    )(x, g, w_in, w_grp, scale, w_out, hist_t)


def _rope(xh, cos_t, sin_t, lane):
    partner = jnp.where(lane < ROT_DIM // 2, pltpu.roll(xh, LANES - ROT_DIM // 2, 1),
                        pltpu.roll(xh, ROT_DIM // 2, 1))
    return xh * cos_t + partner * sin_t


def _qkv_body(n_prompt_tiles, x_ref, g_ref, w_ref, cos_ref, sin_ref,
              q_ref, kp_ref, vp_ref, ks_ref, vs_ref):
    i = pl.program_id(0)
    x = x_ref[...]
    h = _rms(x, g_ref[...]).astype(BF16)
    qkv = _dot(h, w_ref[...])
    cos_t = cos_ref[...]
    sin_t = sin_ref[...]
    lane = lax.broadcasted_iota(jnp.int32, (ROW_TILE, LANES), 1)
    q_parts, k_parts = [], []
    for hd in range(N_HEADS):
        q_parts.append(_rope(qkv[:, hd * HEAD_DIM:(hd + 1) * HEAD_DIM], cos_t, sin_t, lane))
        off = D_MODEL + hd * HEAD_DIM
        k_parts.append(_rope(qkv[:, off:off + HEAD_DIM], cos_t, sin_t, lane))
    q_ref[...] = jnp.concatenate(q_parts, axis=1)
    k = jnp.concatenate(k_parts, axis=1)
    v = qkv[:, 2 * D_MODEL:]

    @pl.when(i < n_prompt_tiles)
    def _():
        kp_ref[...] = k
        vp_ref[...] = v

    @pl.when(i >= n_prompt_tiles)
    def _():
        ks_ref[...] = k
        vs_ref[...] = v


def _qkv(x, g, w_qkv, cos_t, sin_t, n_prompt):
    n = x.shape[0]
    npt = n_prompt // ROW_TILE
    assert n - n_prompt == ROW_TILE
    prompt_spec = pl.BlockSpec((ROW_TILE, D_MODEL), lambda i: (jnp.minimum(i, npt - 1), 0))
    sample_spec = pl.BlockSpec((ROW_TILE, D_MODEL), lambda i: (0, 0))
    return pl.pallas_call(
        functools.partial(_qkv_body, npt),
        grid=(n // ROW_TILE,),
        in_specs=[_row_spec(D_MODEL), _const_spec((1, D_MODEL)), _const_spec((D_MODEL, 3 * D_MODEL)),
                  _row_spec(LANES), _row_spec(LANES)],
        out_specs=[_row_spec(D_MODEL), prompt_spec, prompt_spec, sample_spec, sample_spec],
        out_shape=[jax.ShapeDtypeStruct((n, D_MODEL), F32),
                   jax.ShapeDtypeStruct((n_prompt, D_MODEL), F32),
                   jax.ShapeDtypeStruct((n_prompt, D_MODEL), F32),
                   jax.ShapeDtypeStruct((ROW_TILE, D_MODEL), F32),
                   jax.ShapeDtypeStruct((ROW_TILE, D_MODEL), F32)],
        compiler_params=_params(),
        name="moba_qkv",
    )(x, g, w_qkv, cos_t, sin_t)


def _top_k_mask(gate, valid, k, axis):
    idx = lax.broadcasted_iota(jnp.int32, gate.shape, axis)
    size = gate.shape[axis]
    g = jnp.where(valid, gate, -jnp.inf)
    chosen = jnp.zeros(gate.shape, dtype=jnp.bool_)
    for _ in range(k):
        mx = jnp.max(g, axis=axis, keepdims=True)
        first = jnp.min(jnp.where(g == mx, idx, size), axis=axis, keepdims=True)
        pick = (idx == first) & (mx > -jnp.inf)
        chosen = chosen | pick
        g = jnp.where(pick, -jnp.inf, g)
    return chosen


KV_CHUNK = 4
PREP_BLOCKS = 4


def _column_fold(x, op):
    return op(x.reshape(x.shape[0] // SUBLANES, SUBLANES, x.shape[1]), axis=0)


def _pattn_body(n_blocks, q_ref, k_ref, v_ref, o_ref, kaug_scr, vt_scr, qaug_scr, kmean_scr, s_scr):
    qb = pl.program_id(2)
    exp2_scale = (HEAD_DIM ** -0.5) * math.log2(math.e)
    chunk = KV_CHUNK * MOBA_BLOCK

    @pl.when(qb == 0)
    def _():
        lane = lax.broadcasted_iota(jnp.int32, (MOBA_BLOCK, LANES), 1)

        def fill(j, carry):
            rows = pl.ds(pl.multiple_of(j * MOBA_BLOCK, MOBA_BLOCK), MOBA_BLOCK)
            kj = k_ref[rows, :]
            kaug_scr[rows, 0:HEAD_DIM] = kj.astype(BF16)
            kaug_scr[rows, HEAD_DIM:] = jnp.where(lane == j, 1.0, 0.0).astype(BF16)
            vt_scr[j] = v_ref[rows, :].T.astype(BF16)
            kmean_scr[pl.ds(j, 1), :] = jnp.mean(kj, axis=0, keepdims=True)
            return carry

        lax.fori_loop(0, n_blocks, fill, 0)
        m_hi, m_lo = _split_bf16(kmean_scr[...])
        width = PREP_BLOCKS * MOBA_BLOCK
        blk = lax.broadcasted_iota(jnp.int32, (n_blocks, width), 0)
        col_blk = lax.broadcasted_iota(jnp.int32, (n_blocks, width), 1) // MOBA_BLOCK
        pad = jnp.zeros((LANES - n_blocks, MOBA_BLOCK), BF16)

        def prep(jj, carry):
            rows = pl.ds(pl.multiple_of(jj * width, width), width)
            qt = q_ref[rows, :].T
            qt_hi, qt_lo = _split_bf16(qt)
            gate = _dot(m_hi, qt_hi) + _dot(m_lo, qt_hi) + _dot(m_hi, qt_lo)
            allowed = _top_k_mask(gate, blk < col_blk + jj * PREP_BLOCKS, MOBA_TOP_K, 0)
            bias = jnp.where(allowed, 0.0, NEG_BIG).astype(BF16)
            for i in range(PREP_BLOCKS):
                cols = slice(i * MOBA_BLOCK, (i + 1) * MOBA_BLOCK)
                qaug_scr[jj * PREP_BLOCKS + i] = jnp.concatenate([qt_hi[:, cols], bias[:, cols], pad], axis=0)
            return carry

        lax.fori_loop(0, n_blocks // PREP_BLOCKS, prep, 0)

    q_aug = qaug_scr[qb]
    own = pl.ds(pl.multiple_of(qb * MOBA_BLOCK, MOBA_BLOCK), MOBA_BLOCK)

    def attend(k_chunks):
        s = _dot(kaug_scr[own, 0:HEAD_DIM], q_aug[0:HEAD_DIM, :])
        key = lax.broadcasted_iota(jnp.int32, s.shape, 0)
        qry = lax.broadcasted_iota(jnp.int32, s.shape, 1)
        s = jnp.where(key <= qry, s, NEG_BIG)
        s_scr[0:MOBA_BLOCK, :] = s
        m8 = _column_fold(s, jnp.max)
        for c in range(k_chunks):
            s = _dot(kaug_scr[c * chunk:(c + 1) * chunk, :], q_aug)
            s_scr[MOBA_BLOCK + c * chunk:MOBA_BLOCK + (c + 1) * chunk, :] = s
            m8 = jnp.maximum(m8, _column_fold(s, jnp.max))
        m = jnp.max(m8, axis=0, keepdims=True)
        p = jnp.exp2((s_scr[0:MOBA_BLOCK, :] - m) * exp2_scale)
        l8 = _column_fold(p, jnp.sum)
        acc = _dot(vt_scr[qb], p.astype(BF16))
        for j in range(k_chunks * KV_CHUNK):
            rows = slice((j + 1) * MOBA_BLOCK, (j + 2) * MOBA_BLOCK)
            p = jnp.exp2((s_scr[rows, :] - m) * exp2_scale)
            l8 = l8 + _column_fold(p, jnp.sum)
            acc = acc + _dot(vt_scr[j], p.astype(BF16))
        l = jnp.sum(l8, axis=0, keepdims=True)
        o_ref[...] = (acc * (1.0 / l)).T.astype(o_ref.dtype)

    n_chunks = (qb + KV_CHUNK - 1) // KV_CHUNK
    for k_chunks in range(n_blocks // KV_CHUNK + 1):
        pl.when(n_chunks == k_chunks)(functools.partial(attend, k_chunks))


def _prompt_attention(q_all, k_p, v_p, batch, seq):
    n_blocks = seq // MOBA_BLOCK
    assert seq % MOBA_BLOCK == 0 and n_blocks <= LANES
    assert n_blocks % KV_CHUNK == 0 and n_blocks % PREP_BLOCKS == 0
    assert n_blocks % (2 * SUBLANES) == 0
    qspec = pl.BlockSpec((MOBA_BLOCK, HEAD_DIM), lambda b, h, i: (b * n_blocks + i, h))
    seqspec = pl.BlockSpec((seq, HEAD_DIM), lambda b, h, i: (b, h))
    return pl.pallas_call(
        functools.partial(_pattn_body, n_blocks),
        grid=(batch, N_HEADS, n_blocks),
        in_specs=[seqspec, seqspec, seqspec],
        out_specs=qspec,
        out_shape=jax.ShapeDtypeStruct((batch * seq, D_MODEL), BF16),
        scratch_shapes=[pltpu.VMEM((seq, 2 * LANES), BF16),
                        pltpu.VMEM((n_blocks, HEAD_DIM, MOBA_BLOCK), BF16),
                        pltpu.VMEM((n_blocks, 2 * LANES, MOBA_BLOCK), BF16),
                        pltpu.VMEM((n_blocks, HEAD_DIM), F32),
                        pltpu.VMEM((seq + MOBA_BLOCK, MOBA_BLOCK), F32)],
        compiler_params=_params(3),
        name="moba_prompt_attn",
    )(q_all, k_p, v_p)


def _sattn_body(n_pages, dec_seq, pt_ref, q_ref, kn_ref, vn_ref, ck_hbm, cv_hbm, o_ref,
                kbuf, vbuf, sem, s_scr, kmean_scr):
    n = pl.program_id(0)
    n_seq = pl.num_programs(0)
    slot = n % 2
    scale = HEAD_DIM ** -0.5
    pages_per_block = MOBA_BLOCK // PAGE_SIZE
    n_blocks = n_pages // pages_per_block

    def page_copy(hbm, buf, seq, sl, p, kind):
        return pltpu.make_async_copy(hbm.at[pt_ref[seq * n_pages + p]], buf.at[sl, p], sem.at[kind, sl])

    def start_all(seq, sl):
        for p in range(n_pages):
            page_copy(ck_hbm, kbuf, seq, sl, p, 0).start()
            page_copy(cv_hbm, vbuf, seq, sl, p, 1).start()

    @pl.when(n == 0)
    def _():
        start_all(0, 0)

    @pl.when(n + 1 < n_seq)
    def _():
        start_all(n + 1, 1 - slot)

    for p in range(n_pages):
        page_copy(ck_hbm, kbuf, n, slot, p, 0).wait()
        page_copy(cv_hbm, vbuf, n, slot, p, 1).wait()

    def page_rows(buf, p):
        return jnp.concatenate([buf[slot, p, pl.ds(hd, PAGE_SIZE, stride=N_HEADS), :]
                                for hd in range(N_HEADS)], axis=1)

    pad_rows = jnp.zeros((PAGE_SIZE - dec_seq, D_MODEL), F32)
    q_pad = jnp.concatenate([q_ref[0], pad_rows], axis=0)
    row_t = lax.broadcasted_iota(jnp.int32, (PAGE_SIZE, LANES), 0)
    col_c = lax.broadcasted_iota(jnp.int32, (PAGE_SIZE, LANES), 1)
    pick = jnp.where((col_c % dec_seq == row_t) & (col_c < N_HEADS * dec_seq), 1.0, 0.0).astype(BF16)
    head_r = lax.broadcasted_iota(jnp.int32, (D_MODEL, LANES), 0) // HEAD_DIM
    head_c = lax.broadcasted_iota(jnp.int32, (D_MODEL, LANES), 1) // dec_seq
    same_head = head_r == head_c
    q_hi, q_lo = _split_bf16(q_pad)
    qbd_hi = jnp.where(same_head, _dot_t0(q_hi, pick), 0.0).astype(BF16)
    qbd_lo = jnp.where(same_head, _dot_t0(q_lo, pick), 0.0).astype(BF16)

    kmean_scr[...] = jnp.zeros(kmean_scr.shape, F32)
    for b in range(n_blocks):
        ksum = None
        for r in range(pages_per_block):
            p = b * pages_per_block + r
            kp = page_rows(kbuf, p)
            s_scr[p * PAGE_SIZE:(p + 1) * PAGE_SIZE, :] = _dot(kp.astype(BF16), qbd_hi) * scale
            part = jnp.sum(kp, axis=0, keepdims=True)
            ksum = part if ksum is None else ksum + part
        kmean_scr[b:b + 1, :] = ksum * (1.0 / MOBA_BLOCK)
    m_hi, m_lo = _split_bf16(kmean_scr[...])
    gate = _dot(m_hi, qbd_hi) + _dot(m_lo, qbd_hi) + _dot(m_hi, qbd_lo)
    blk = lax.broadcasted_iota(jnp.int32, gate.shape, 0)
    allowed = _top_k_mask(gate, blk < n_blocks, min(MOBA_TOP_K, n_blocks), 0)
    bias = jnp.where(allowed, 0.0, NEG_BIG)

    k_pad = jnp.concatenate([kn_ref[0], pad_rows], axis=0)
    v_pad = jnp.concatenate([vn_ref[0], pad_rows], axis=0)
    s_new = _dot(k_pad.astype(BF16), qbd_hi) * scale
    s_new = jnp.where((row_t <= col_c % dec_seq) & (row_t < dec_seq), s_new, NEG_BIG)

    m = jnp.max(s_new, axis=0, keepdims=True)
    for b in range(n_blocks):
        rows = slice(b * MOBA_BLOCK, (b + 1) * MOBA_BLOCK)
        sb = s_scr[rows, :] + bias[b:b + 1, :]
        s_scr[rows, :] = sb
        m = jnp.maximum(m, jnp.max(sb, axis=0, keepdims=True))
    p_new = jnp.exp(s_new - m)
    l = jnp.sum(p_new, axis=0, keepdims=True)
    for p in range(n_pages):
        rows = slice(p * PAGE_SIZE, (p + 1) * PAGE_SIZE)
        pp = jnp.exp(s_scr[rows, :] - m)
        s_scr[rows, :] = pp
        l = l + jnp.sum(pp, axis=0, keepdims=True)
    inv = 1.0 / l
    out = _dot_t0((p_new * inv).astype(BF16), v_pad.astype(BF16))
    for p in range(n_pages):
        rows = slice(p * PAGE_SIZE, (p + 1) * PAGE_SIZE)
        out = out + _dot_t0((s_scr[rows, :] * inv).astype(BF16), page_rows(vbuf, p).astype(BF16))
    out_c = lax.broadcasted_iota(jnp.int32, out.shape, 0)
    out_h = lax.broadcasted_iota(jnp.int32, out.shape, 1) // HEAD_DIM
    for t in range(dec_seq):
        keep = (out_c % dec_seq == t) & (out_c // dec_seq == out_h)
        o_ref[0, t:t + 1, :] = jnp.sum(jnp.where(keep, out, 0.0), axis=0, keepdims=True)


def _sample_attention(page_table, q_s, k_s, v_s, cache_k, cache_v):
    n_seq, n_pages = page_table.shape
    dec_seq = q_s.shape[1]
    assert n_pages % (MOBA_BLOCK // PAGE_SIZE) == 0 and N_HEADS * dec_seq <= LANES and dec_seq <= SUBLANES
    row_spec = pl.BlockSpec((1, dec_seq, D_MODEL), lambda n, pt: (n, 0, 0))
    any_spec = pl.BlockSpec(memory_space=pl.ANY)
    grid_spec = pltpu.PrefetchScalarGridSpec(
        num_scalar_prefetch=1,
        grid=(n_seq,),
        in_specs=[row_spec, row_spec, row_spec, any_spec, any_spec],
        out_specs=row_spec,
        scratch_shapes=[pltpu.VMEM((2, n_pages, PAGE_SIZE * N_HEADS, HEAD_DIM), F32),
                        pltpu.VMEM((2, n_pages, PAGE_SIZE * N_HEADS, HEAD_DIM), F32),
                        pltpu.SemaphoreType.DMA((2, 2)),
                        pltpu.VMEM((n_pages * PAGE_SIZE, LANES), F32),
                        pltpu.VMEM((2 * SUBLANES, D_MODEL), F32)],
    )
    return pl.pallas_call(
        functools.partial(_sattn_body, n_pages, dec_seq),
        grid_spec=grid_spec,
        out_shape=jax.ShapeDtypeStruct((n_seq, dec_seq, D_MODEL), F32),
        compiler_params=_params(),
        name="moba_sample_attn",
    )(page_table.reshape(-1), q_s, k_s, v_s, cache_k, cache_v)


def _oproj_body(n_prompt_tiles, x_ref, op_ref, os_ref, w_ref, o_ref):
    i = pl.program_id(0)
    o = jnp.where(i < n_prompt_tiles, op_ref[...], os_ref[...])
    o_ref[...] = x_ref[...] + _dot(o, w_ref[...])


def _out_proj(x, o_p, o_s, w_out, n_prompt):
    n = x.shape[0]
    npt = n_prompt // ROW_TILE
    return pl.pallas_call(
        functools.partial(_oproj_body, npt),
        grid=(n // ROW_TILE,),
        in_specs=[_row_spec(D_MODEL),
                  pl.BlockSpec((ROW_TILE, D_MODEL), lambda i: (jnp.minimum(i, npt - 1), 0)),
                  pl.BlockSpec((ROW_TILE, D_MODEL), lambda i: (0, 0)),
                  _const_spec((D_MODEL, D_MODEL))],
        out_specs=_row_spec(D_MODEL),
        out_shape=jax.ShapeDtypeStruct((n, D_MODEL), F32),
        compiler_params=_params(),
        name="moba_out_proj",
    )(x, o_p, o_s, w_out)


def _rope_tables(seq, n_prompt, past_len, dec_batch, dec_seq):
    half = ROT_DIM // 2
    pos = jnp.concatenate([jnp.arange(n_prompt, dtype=jnp.int32) % seq,
                           past_len + jnp.repeat(jnp.arange(dec_seq, dtype=jnp.int32), dec_batch)])
    inv_freq = ROPE_THETA ** (-jnp.arange(half, dtype=F32) / half)
    ang = pos.astype(F32)[:, None] * inv_freq[None, :]
    cos, sin = jnp.cos(ang), jnp.sin(ang)
    ones = jnp.ones((pos.shape[0], LANES - ROT_DIM), F32)
    cos_t = jnp.concatenate([cos, cos, ones], axis=1)
    sin_t = jnp.concatenate([-sin, sin, 0.0 * ones], axis=1)
    return cos_t, sin_t


def _to_time_major(a):
    return jnp.swapaxes(a, 0, 1).reshape((a.shape[0] * a.shape[1],) + a.shape[2:])


def _from_time_major(a, dec_batch, dec_seq):
    return jnp.swapaxes(a.reshape((dec_seq, dec_batch) + a.shape[1:]), 0, 1)


def kernel(x_prompt, x_sample, cache_k, cache_v, state_pool, page_table, p_prompt, p_sample, norm_ffn1, norm_mix, norm_ffn2, norm_ple, norm_final, ffn1_w_gate, ffn1_w_up, ffn1_w_down, ffn2_w_gate, ffn2_w_up, ffn2_w_down, ple_w_gate, ple_w_proj, a_w_in, a_ln_g, a_ln_b, a_w_s, a_b_s, a_w_out, b_w_qkv, b_w_out, c_w_in, c_w_grp, c_scale, c_w_out):
    batch, seq, d = x_prompt.shape
    dec_batch, dec_seq, _ = x_sample.shape
    n_prompt = batch * seq
    n_sample = dec_batch * dec_seq
    n_pages = page_table.shape[1]
    past_len = n_pages * PAGE_SIZE
    assert d == D_MODEL and n_sample == ROW_TILE and n_prompt % ROW_TILE == 0
    assert past_len % MOBA_BLOCK == 0

    x = (x_prompt.reshape(n_prompt, d), _to_time_major(x_sample))
    p_rows_prompt = p_prompt.reshape(DEPTH, n_prompt, PLE_DIM)
    p_rows_sample = jnp.swapaxes(p_sample, 1, 2).reshape(DEPTH, n_sample, PLE_DIM)
    row = lambda v: v.reshape(1, -1)
    bf = lambda w: w.astype(BF16)

    k_p, v_p, k_s, v_s, pool_p, pool_s, chunk_v = [], [], [], [], [], [], []
    for i in range(DEPTH):
        kind, j = LAYER_MIXERS[i], LAYER_SLOT[i]
        x = _ffn(x, row(norm_ffn1[i]), bf(ffn1_w_gate[i]), bf(ffn1_w_up[i]), bf(ffn1_w_down[i]), n_prompt)
        g_mix = row(norm_mix[i])
        if kind == 0:
            bexp = jnp.repeat(a_b_s[j].T, LANES, axis=1)
            wrow = jnp.repeat(jnp.transpose(a_w_s[j][:, :dec_seq, :dec_seq], (1, 2, 0)), LANES, axis=2)
            brow = jnp.repeat(a_b_s[j][:, :dec_seq].T, LANES, axis=1)
            x, vs = _mixer_a(x, g_mix, bf(a_w_in[j]), row(a_ln_g[j]), row(a_ln_b[j]), a_w_s[j], bexp,
                             wrow, brow, bf(a_w_out[j]), n_prompt, dec_batch, dec_seq)
            chunk_v.append(_from_time_major(vs, dec_batch, dec_seq))
        elif kind == 1:
            cos_t, sin_t = _rope_tables(seq, n_prompt, past_len, dec_batch, dec_seq)
            q_all, kp, vp, ks, vs = _qkv(x, g_mix, bf(b_w_qkv[j]), cos_t, sin_t, n_prompt)
            o_p = _prompt_attention(q_all, kp, vp, batch, seq)
            q_s = _from_time_major(q_all[n_prompt:], dec_batch, dec_seq)
            ks_n = _from_time_major(ks, dec_batch, dec_seq)
            vs_n = _from_time_major(vs, dec_batch, dec_seq)
            o_s = _sample_attention(page_table, q_s, ks_n, vs_n,
                                    cache_k[j].reshape(-1, PAGE_SIZE * N_HEADS, HEAD_DIM),
                                    cache_v[j].reshape(-1, PAGE_SIZE * N_HEADS, HEAD_DIM))
            o_s = _to_time_major(o_s).astype(BF16)
            x = _out_proj(x, o_p, o_s, bf(b_w_out[j]), n_prompt)
            k_p.append(kp.reshape(batch, seq, N_HEADS, HEAD_DIM))
            v_p.append(vp.reshape(batch, seq, N_HEADS, HEAD_DIM))
            k_s.append(ks_n.reshape(dec_batch, dec_seq, N_HEADS, HEAD_DIM))
            v_s.append(vs_n.reshape(dec_batch, dec_seq, N_HEADS, HEAD_DIM))
        else:
            hist_t = jnp.swapaxes(state_pool[j], 0, 1)
            x, tails, us = _mixer_c(x, g_mix, bf(c_w_in[j]), bf(c_w_grp[j]), row(c_scale[j]), bf(c_w_out[j]),
                                    hist_t, n_prompt, seq, dec_batch, dec_seq)
            tps = seq // ROW_TILE
            pool_p.append(tails[tps - 1:batch * tps:tps, HIST_PAD - POOL_HIST:, :])
            u_new = _from_time_major(us, dec_batch, dec_seq)
            pool_s.append(jnp.concatenate([state_pool[j], u_new], axis=1)[:, -POOL_HIST:])
        ple = (row(norm_ple[i]), bf(ple_w_gate[i]), p_rows_prompt, p_rows_sample, bf(ple_w_proj[i]), i)
        x = _ffn(x, row(norm_ffn2[i]), bf(ffn2_w_gate[i]), bf(ffn2_w_up[i]), bf(ffn2_w_down[i]), n_prompt,
                 ple=ple, final_gain=row(norm_final) if i + 1 == DEPTH else None)
    y_prompt, y_sample = x
    y_prompt = y_prompt.reshape(batch, seq, d)
    y_sample = _from_time_major(y_sample, dec_batch, dec_seq)
    return (y_prompt, y_sample, jnp.stack(k_p), jnp.stack(v_p), jnp.stack(k_s), jnp.stack(v_s),
            jnp.stack(pool_p), jnp.stack(pool_s), jnp.stack(chunk_v))
```

```python
import functools
import math

import jax
import jax.numpy as jnp
import numpy as np
from jax import lax
from jax.experimental import pallas as pl
from jax.experimental.pallas import tpu as pltpu

F32 = jnp.float32
BF16 = jnp.bfloat16

D_MODEL = 1024
DEPTH = 4
D_FF = 2816
PLE_DIM = 256
RMS_EPS = 1e-6
LN_EPS = 1e-5
CHUNK = 128
A_GROUPS = 8
N_HEADS = 8
HEAD_DIM = 128
ROT_DIM = 32
ROPE_THETA = 500000.0
MOBA_BLOCK = 256
MOBA_TOP_K = 3
PAGE_SIZE = 128
POOL_WINDOWS = (2, 4, 8, 16)
C_GROUP_DIM = 256
POOL_HIST = 15
LAYER_MIXERS = (0, 1, 2, 0)
LAYER_SLOT = (0, 0, 0, 1)

LANES = 128
SUBLANES = 8
VMEM_LIMIT = 56 * 1024 * 1024

ROW_TILE = 512
NEG_BIG = -1e30


def _dot(a, b):
    return jnp.dot(a, b, preferred_element_type=F32)


def _dot_t0(a, b):
    return lax.dot_general(a, b, (((0,), (0,)), ((), ())), preferred_element_type=F32)


def _dot_t1(a, b):
    return lax.dot_general(a, b, (((1,), (1,)), ((), ())), preferred_element_type=F32)


def _rms(x, g):
    return x * lax.rsqrt(jnp.mean(x * x, axis=-1, keepdims=True) + RMS_EPS) * g


def _split_bf16(x):
    hi = x.astype(BF16)
    lo = (x - hi.astype(F32)).astype(BF16)
    return hi, lo


def _const_spec(shape):
    nd = len(shape)
    return pl.BlockSpec(shape, lambda *_: (0,) * nd, pipeline_mode=pl.Buffered(1))


def _layer_spec(shape, layer):
    nd = len(shape)
    return pl.BlockSpec((None,) + tuple(shape), lambda *_: (layer,) + (0,) * nd, pipeline_mode=pl.Buffered(1))


def _row_spec(width):
    return pl.BlockSpec((ROW_TILE, width), lambda i: (i, 0))


def _params(n_axes=1):
    return pltpu.CompilerParams(dimension_semantics=("arbitrary",) * n_axes,
                                vmem_limit_bytes=VMEM_LIMIT)


def _ffn_body(n_prompt_tiles, split_x, attn, ple, final, *refs):
    refs = list(refs)
    is_prompt = pl.program_id(0) < n_prompt_tiles
    if split_x:
        xp_ref, xs_ref = refs[:2]
        refs = refs[2:]
        x = jnp.where(is_prompt, xp_ref[...], xs_ref[...])
    else:
        x = refs.pop(0)[...]
    if attn:
        op_ref, os_ref, wo_ref = refs[:3]
        refs = refs[3:]
        x = x + _dot(jnp.where(is_prompt, op_ref[...], os_ref[...]), wo_ref[...])
    g_ref, wg_ref, wu_ref, wd_ref = refs[:4]
    refs = refs[4:]
    h = _rms(x, g_ref[...]).astype(BF16)
    gate = _dot(h, wg_ref[...])
    up = _dot(h, wu_ref[...])
    a = (gate * jax.nn.sigmoid(gate) * up).astype(BF16)
    x = x + 0.5 * _dot(a, wd_ref[...])
    if ple:
        gp_ref, wgate_ref, pp_ref, ps_ref, wproj_ref = refs[:5]
        refs = refs[5:]
        p = jnp.where(is_prompt, pp_ref[...], ps_ref[...]).astype(BF16)
        h = _rms(x, gp_ref[...]).astype(BF16)
        x = x + jax.nn.sigmoid(_dot(h, wgate_ref[...])) * _dot(p, wproj_ref[...])
    if not final:
        (o_ref,) = refs
        o_ref[...] = x
        return
    gf_ref, yp_ref, ys_ref = refs
    y = _rms(x, gf_ref[...])

    @pl.when(is_prompt)
    def _():
        yp_ref[...] = y

    @pl.when(jnp.logical_not(is_prompt))
    def _():
        ys_ref[...] = y


def _ffn(x, layer, gains, wg, wu, wd, n_prompt, attn=None, ple=None, final_gain=None):
    npt = n_prompt // ROW_TILE
    prompt_rows = lambda i: (jnp.minimum(i, npt - 1), 0)
    sample_rows = lambda i: (0, 0)
    split_x = isinstance(x, tuple)
    if split_x:
        args = list(x)
        in_specs = [pl.BlockSpec((ROW_TILE, D_MODEL), prompt_rows), pl.BlockSpec((ROW_TILE, D_MODEL), sample_rows)]
        n = n_prompt + x[1].shape[0]
    else:
        args = [x]
        in_specs = [_row_spec(D_MODEL)]
        n = x.shape[0]
    assert n - n_prompt == ROW_TILE
    if attn is not None:
        o_p, o_s, w_out, slot = attn
        args += [o_p, o_s, w_out]
        in_specs += [pl.BlockSpec((ROW_TILE, D_MODEL), prompt_rows), pl.BlockSpec((ROW_TILE, D_MODEL), sample_rows),
                     _layer_spec((D_MODEL, D_MODEL), slot)]
    args += [gains, wg, wu, wd]
    in_specs += [_layer_spec((1, D_MODEL), layer), _layer_spec((D_MODEL, D_FF), layer),
                 _layer_spec((D_MODEL, D_FF), layer), _layer_spec((D_FF, D_MODEL), layer)]
    if ple is not None:
        gp, wgate, p_prompt, p_sample, wproj = ple
        args += [gp, wgate, p_prompt, p_sample, wproj]
        in_specs += [_layer_spec((1, D_MODEL), layer), _layer_spec((D_MODEL, D_MODEL), layer),
                     pl.BlockSpec((None, ROW_TILE, PLE_DIM), lambda i: (layer, jnp.minimum(i, npt - 1), 0)),
                     pl.BlockSpec((None, ROW_TILE, PLE_DIM), lambda i: (layer, 0, 0)),
                     _layer_spec((PLE_DIM, D_MODEL), layer)]
    if final_gain is None:
        out_specs = _row_spec(D_MODEL)
        out_shape = jax.ShapeDtypeStruct((n, D_MODEL), F32)
    else:
        args.append(final_gain)
        in_specs.append(_const_spec((1, D_MODEL)))
        out_specs = [pl.BlockSpec((ROW_TILE, D_MODEL), prompt_rows), pl.BlockSpec((ROW_TILE, D_MODEL), sample_rows)]
        out_shape = [jax.ShapeDtypeStruct((n_prompt, D_MODEL), F32), jax.ShapeDtypeStruct((ROW_TILE, D_MODEL), F32)]
    return pl.pallas_call(
        functools.partial(_ffn_body, npt, split_x, attn is not None, ple is not None, final_gain is not None),
        grid=(n // ROW_TILE,),
        in_specs=in_specs,
        out_specs=out_specs,
        out_shape=out_shape,
        compiler_params=_params(),
        name="ffn_ple" if ple is not None else "ffn",
    )(*args)


def _mixa_body(n_prompt_tiles, dec_batch, dec_seq,
               x_ref, g_ref, win_ref, lng_ref, lnb_ref, ws_ref, bexp_ref, wrow_ref, brow_ref, wout_ref,
               o_ref, vs_ref, v_scr, m_scr):
    i = pl.program_id(0)
    x = x_ref[...]
    h = _rms(x, g_ref[...]).astype(BF16)
    z = _dot(h, win_ref[...])
    z = 0.5 * z * (1.0 + lax.erf(z * (1.0 / math.sqrt(2.0))))
    u = z[:, :D_MODEL]
    v = z[:, D_MODEL:]
    mu = jnp.mean(v, axis=-1, keepdims=True)
    vc = v - mu
    var = jnp.mean(vc * vc, axis=-1, keepdims=True)
    v_scr[...] = vc * lax.rsqrt(var + LN_EPS) * lng_ref[...] + lnb_ref[...]
    n_chunks = ROW_TILE // CHUNK

    @pl.when(i < n_prompt_tiles)
    def _():
        rows = lax.broadcasted_iota(jnp.int32, (CHUNK, CHUNK), 0)
        cols = lax.broadcasted_iota(jnp.int32, (CHUNK, CHUNK), 1)
        causal = cols <= rows
        for g in range(A_GROUPS):
            gs = slice(g * LANES, (g + 1) * LANES)
            w = jnp.where(causal, ws_ref[g], 0.0).astype(BF16)
            rhs = jnp.concatenate([v_scr[c * CHUNK:(c + 1) * CHUNK, gs] for c in range(n_chunks)],
                                  axis=1).astype(BF16)
            mg = _dot(w, rhs)
            for c in range(n_chunks):
                m_scr[c * CHUNK:(c + 1) * CHUNK, gs] = mg[:, c * LANES:(c + 1) * LANES] + bexp_ref[:, gs]

    @pl.when(i >= n_prompt_tiles)
    def _():
        for t in range(dec_seq):
            acc = brow_ref[t:t + 1, :] + wrow_ref[t, 0:1, :] * v_scr[0:dec_batch, :]
            for s in range(1, t + 1):
                acc = acc + wrow_ref[t, s:s + 1, :] * v_scr[s * dec_batch:(s + 1) * dec_batch, :]
            m_scr[t * dec_batch:(t + 1) * dec_batch, :] = acc
        vs_ref[...] = v_scr[...]

    y = _dot((u * m_scr[...]).astype(BF16), wout_ref[...])
    o_ref[...] = x + y


def _mixer_a(x, layer, slot, g, w_in, ln_g, ln_b, w_s, bexp, wrow, brow, w_out, n_prompt, dec_batch, dec_seq):
    n = x.shape[0]
    assert n - n_prompt == ROW_TILE == dec_batch * dec_seq and dec_batch % SUBLANES == 0
    body = functools.partial(_mixa_body, n_prompt // ROW_TILE, dec_batch, dec_seq)
    return pl.pallas_call(
        body,
        grid=(n // ROW_TILE,),
        in_specs=[_row_spec(D_MODEL), _layer_spec((1, D_MODEL), layer), _layer_spec((D_MODEL, 2 * D_MODEL), slot),
                  _layer_spec((1, D_MODEL), slot), _layer_spec((1, D_MODEL), slot),
                  _layer_spec((A_GROUPS, CHUNK, CHUNK), slot), _const_spec((CHUNK, D_MODEL)),
                  _const_spec((dec_seq, dec_seq, D_MODEL)), _const_spec((dec_seq, D_MODEL)),
                  _layer_spec((D_MODEL, D_MODEL), slot)],
        out_specs=[_row_spec(D_MODEL), pl.BlockSpec((ROW_TILE, D_MODEL), lambda i: (0, 0))],
        out_shape=[jax.ShapeDtypeStruct((n, D_MODEL), F32),
                   jax.ShapeDtypeStruct((ROW_TILE, D_MODEL), F32)],
        scratch_shapes=[pltpu.VMEM((ROW_TILE, D_MODEL), F32), pltpu.VMEM((ROW_TILE, D_MODEL), F32)],
        compiler_params=_params(),
        name="mixer_a",
    )(x, g, w_in, ln_g, ln_b, w_s, bexp, wrow, brow, w_out)


HIST_PAD = 16


def _mixc_body(n_prompt_tiles, tiles_per_seq, dec_batch, dec_seq,
               x_ref, g_ref, win_ref, wgrp_ref, scale_ref, wout_ref, hist_ref,
               o_ref, tail_ref, us_ref, ext_scr, pool_scr):
    i = pl.program_id(0)
    x = x_ref[...]
    h = _rms(x, g_ref[...]).astype(BF16)
    u = _dot(h, win_ref[...])

    @pl.when(i < n_prompt_tiles)
    def _():
        @pl.when(i % tiles_per_seq == 0)
        def _():
            ext_scr[0:HIST_PAD, :] = jnp.zeros((HIST_PAD, D_MODEL), F32)

        ext_scr[HIST_PAD:, :] = u
        pos = (i % tiles_per_seq) * ROW_TILE + lax.broadcasted_iota(jnp.int32, (ROW_TILE, 1), 0)
        for g, win in enumerate(POOL_WINDOWS):
            gs = slice(g * C_GROUP_DIM, (g + 1) * C_GROUP_DIM)
            tot = ext_scr[HIST_PAD:, gs]
            for d in range(1, win):
                tot = tot + ext_scr[HIST_PAD - d:HIST_PAD - d + ROW_TILE, gs]
            inv = 1.0 / jnp.minimum(pos + 1, win).astype(F32)
            pool_scr[:, gs] = tot * inv - ext_scr[HIST_PAD:, gs]
        tail = ext_scr[ROW_TILE:, :]
        tail_ref[0] = tail
        ext_scr[0:HIST_PAD, :] = tail

    @pl.when(i >= n_prompt_tiles)
    def _():
        ext_scr[0:ROW_TILE, :] = u
        us_ref[...] = u
        tail_ref[0] = u[ROW_TILE - HIST_PAD:, :]

        def ext_row(j, gs):
            if j < POOL_HIST:
                return hist_ref[j, :, gs]
            return ext_scr[(j - POOL_HIST) * dec_batch:(j - POOL_HIST + 1) * dec_batch, gs]

        for t in range(dec_seq):
            for g, win in enumerate(POOL_WINDOWS):
                gs = slice(g * C_GROUP_DIM, (g + 1) * C_GROUP_DIM)
                cur = ext_row(POOL_HIST + t, gs)
                tot = cur
                for d in range(1, win):
                    tot = tot + ext_row(POOL_HIST + t - d, gs)
                pool_scr[t * dec_batch:(t + 1) * dec_batch, gs] = tot * (1.0 / win) - cur

    parts = []
    for g in range(len(POOL_WINDOWS)):
        gs = slice(g * C_GROUP_DIM, (g + 1) * C_GROUP_DIM)
        parts.append(_dot(pool_scr[:, gs].astype(BF16), wgrp_ref[g]))
    mixed = jnp.concatenate(parts, axis=1) * scale_ref[...]
    o_ref[...] = x + _dot(mixed.astype(BF16), wout_ref[...])


def _mixer_c(x, layer, slot, g, w_in, w_grp, scale, w_out, hist_t, n_prompt, seq, dec_batch, dec_seq):
    n = x.shape[0]
    nt = n // ROW_TILE
    assert n - n_prompt == ROW_TILE == dec_batch * dec_seq and seq % ROW_TILE == 0
    assert POOL_HIST <= HIST_PAD and max(POOL_WINDOWS) - 1 <= POOL_HIST
    body = functools.partial(_mixc_body, n_prompt // ROW_TILE, seq // ROW_TILE, dec_batch, dec_seq)
    return pl.pallas_call(
        body,
        grid=(nt,),
        in_specs=[_row_spec(D_MODEL), _layer_spec((1, D_MODEL), layer), _layer_spec((D_MODEL, D_MODEL), slot),
                  _layer_spec((len(POOL_WINDOWS), C_GROUP_DIM, C_GROUP_DIM), slot), _layer_spec((1, D_MODEL), slot),
                  _layer_spec((D_MODEL, D_MODEL), slot), _const_spec((POOL_HIST, dec_batch, D_MODEL))],
        out_specs=[_row_spec(D_MODEL),
                   pl.BlockSpec((1, HIST_PAD, D_MODEL), lambda i: (i, 0, 0)),
                   pl.BlockSpec((ROW_TILE, D_MODEL), lambda i: (0, 0))],
        out_shape=[jax.ShapeDtypeStruct((n, D_MODEL), F32),
                   jax.ShapeDtypeStruct((nt, HIST_PAD, D_MODEL), F32),
                   jax.ShapeDtypeStruct((ROW_TILE, D_MODEL), F32)],
        scratch_shapes=[pltpu.VMEM((HIST_PAD + ROW_TILE, D_MODEL), F32),
                        pltpu.VMEM((ROW_TILE, D_MODEL), F32)],
        compiler_params=_params(),
        name="mixer_c",
    )(x, g, w_in, w_grp, scale, w_out, hist_t)


def _rope(xh, cos_t, sin_t, lane):
    partner = jnp.where(lane < ROT_DIM // 2, pltpu.roll(xh, LANES - ROT_DIM // 2, 1),
                        pltpu.roll(xh, ROT_DIM // 2, 1))
    return xh * cos_t + partner * sin_t


def _qkv_body(n_prompt_tiles, x_ref, g_ref, w_ref, cos_ref, sin_ref,
              q_ref, kp_ref, vp_ref, ks_ref, vs_ref):
    i = pl.program_id(0)
    x = x_ref[...]
    h = _rms(x, g_ref[...]).astype(BF16)
    qkv = _dot(h, w_ref[...])
    cos_t = cos_ref[...]
    sin_t = sin_ref[...]
    lane = lax.broadcasted_iota(jnp.int32, (ROW_TILE, LANES), 1)
    q_parts, k_parts = [], []
    for hd in range(N_HEADS):
        q_parts.append(_rope(qkv[:, hd * HEAD_DIM:(hd + 1) * HEAD_DIM], cos_t, sin_t, lane))
        off = D_MODEL + hd * HEAD_DIM
        k_parts.append(_rope(qkv[:, off:off + HEAD_DIM], cos_t, sin_t, lane))
    q_ref[...] = jnp.concatenate(q_parts, axis=1)
    k = jnp.concatenate(k_parts, axis=1)
    v = qkv[:, 2 * D_MODEL:]

    @pl.when(i < n_prompt_tiles)
    def _():
        kp_ref[...] = k
        vp_ref[...] = v

    @pl.when(i >= n_prompt_tiles)
    def _():
        ks_ref[...] = k
        vs_ref[...] = v


def _qkv(x, layer, slot, g, w_qkv, cos_t, sin_t, n_prompt, seq):
    n = x.shape[0]
    npt = n_prompt // ROW_TILE
    tps = seq // ROW_TILE
    assert n - n_prompt == ROW_TILE and cos_t.shape[0] == seq + ROW_TILE
    prompt_spec = pl.BlockSpec((ROW_TILE, D_MODEL), lambda i: (jnp.minimum(i, npt - 1), 0))
    sample_spec = pl.BlockSpec((ROW_TILE, D_MODEL), lambda i: (0, 0))
    rope_spec = pl.BlockSpec((ROW_TILE, LANES), lambda i: (jnp.where(i < npt, i % tps, tps), 0))
    return pl.pallas_call(
        functools.partial(_qkv_body, npt),
        grid=(n // ROW_TILE,),
        in_specs=[_row_spec(D_MODEL), _layer_spec((1, D_MODEL), layer), _layer_spec((D_MODEL, 3 * D_MODEL), slot),
                  rope_spec, rope_spec],
        out_specs=[_row_spec(D_MODEL), prompt_spec, prompt_spec, sample_spec, sample_spec],
        out_shape=[jax.ShapeDtypeStruct((n, D_MODEL), F32),
                   jax.ShapeDtypeStruct((n_prompt, D_MODEL), F32),
                   jax.ShapeDtypeStruct((n_prompt, D_MODEL), F32),
                   jax.ShapeDtypeStruct((ROW_TILE, D_MODEL), F32),
                   jax.ShapeDtypeStruct((ROW_TILE, D_MODEL), F32)],
        compiler_params=_params(),
        name="moba_qkv",
    )(x, g, w_qkv, cos_t, sin_t)


def _top_k_mask(gate, valid, k, axis):
    idx = lax.broadcasted_iota(jnp.int32, gate.shape, axis)
    size = gate.shape[axis]
    g = jnp.where(valid, gate, -jnp.inf)
    chosen = jnp.zeros(gate.shape, dtype=jnp.bool_)
    for _ in range(k):
        mx = jnp.max(g, axis=axis, keepdims=True)
        first = jnp.min(jnp.where(g == mx, idx, size), axis=axis, keepdims=True)
        pick = (idx == first) & (mx > -jnp.inf)
        chosen = chosen | pick
        g = jnp.where(pick, -jnp.inf, g)
    return chosen


KV_CHUNK = 4
PREP_BLOCKS = 4
SUM_ROWS = 16


def _column_fold(x, op):
    return op(x.reshape(x.shape[0] // SUBLANES, SUBLANES, x.shape[1]), axis=0)


def _pattn_body(n_blocks, q_ref, k_ref, v_ref, o_ref, kaug_scr, vt_scr, qaug_scr, kmean_scr, s_scr):
    qp = pl.program_id(2)
    exp2_scale = (HEAD_DIM ** -0.5) * math.log2(math.e)
    chunk = KV_CHUNK * MOBA_BLOCK

    @pl.when(qp == 0)
    def _():
        lane = lax.broadcasted_iota(jnp.int32, (MOBA_BLOCK, LANES), 1)
        sum_rows = jnp.where(lax.broadcasted_iota(jnp.int32, (SUM_ROWS, MOBA_BLOCK), 0) == 0, 1.0, 0.0).astype(BF16)

        def fill(j, carry):
            rows = pl.ds(pl.multiple_of(j * MOBA_BLOCK, MOBA_BLOCK), MOBA_BLOCK)
            kj = k_ref[rows, :]
            kaug_scr[rows, 0:HEAD_DIM] = kj.astype(BF16)
            kaug_scr[rows, HEAD_DIM:] = jnp.where(lane == j, 1.0, 0.0).astype(BF16)
            vt_scr[j] = jnp.concatenate([v_ref[rows, :].T.astype(BF16), sum_rows], axis=0)
            kmean_scr[pl.ds(j, 1), :] = jnp.mean(kj, axis=0, keepdims=True)
            return carry

        lax.fori_loop(0, n_blocks, fill, 0)
        m_hi, m_lo = _split_bf16(kmean_scr[...])
        width = PREP_BLOCKS * MOBA_BLOCK
        blk = lax.broadcasted_iota(jnp.int32, (n_blocks, width), 0)
        col_blk = lax.broadcasted_iota(jnp.int32, (n_blocks, width), 1) // MOBA_BLOCK
        pad = jnp.zeros((LANES - n_blocks, MOBA_BLOCK), BF16)

        def prep(jj, carry):
            rows = pl.ds(pl.multiple_of(jj * width, width), width)
            qt = q_ref[rows, :].T
            qt_hi, qt_lo = _split_bf16(qt)
            gate = _dot(m_hi, qt_hi) + _dot(m_lo, qt_hi) + _dot(m_hi, qt_lo)
            allowed = _top_k_mask(gate, blk < col_blk + jj * PREP_BLOCKS, MOBA_TOP_K, 0)
            bias = jnp.where(allowed, 0.0, NEG_BIG).astype(BF16)
            qt_op = (qt * exp2_scale).astype(BF16)
            for i in range(PREP_BLOCKS):
                cols = slice(i * MOBA_BLOCK, (i + 1) * MOBA_BLOCK)
                qaug_scr[jj * PREP_BLOCKS + i] = jnp.concatenate([qt_op[:, cols], bias[:, cols], pad], axis=0)
            return carry

        lax.fori_loop(0, n_blocks // PREP_BLOCKS, prep, 0)

    blk_a = 2 * qp
    q_aug = jnp.concatenate([qaug_scr[blk_a], qaug_scr[blk_a + 1]], axis=1)
    own_a = pl.ds(pl.multiple_of(blk_a * MOBA_BLOCK, MOBA_BLOCK), MOBA_BLOCK)
    own_b = pl.ds(pl.multiple_of((blk_a + 1) * MOBA_BLOCK, MOBA_BLOCK), MOBA_BLOCK)

    def attend(k_chunks):
        def score(c):
            s_scr[c % 2] = _dot(kaug_scr[c * chunk:(c + 1) * chunk, :], q_aug)

        def values(c, pb):
            pv = _dot(vt_scr[c * KV_CHUNK], pb[0:MOBA_BLOCK, :])
            for j in range(1, KV_CHUNK):
                pv = pv + _dot(vt_scr[c * KV_CHUNK + j], pb[j * MOBA_BLOCK:(j + 1) * MOBA_BLOCK, :])
            return pv

        score(0)
        if k_chunks > 1:
            score(1)
        s_own = jnp.concatenate(
            [_dot(kaug_scr[own_a, 0:HEAD_DIM], q_aug[0:HEAD_DIM, 0:MOBA_BLOCK]),
             _dot(kaug_scr[own_b, 0:HEAD_DIM], q_aug[0:HEAD_DIM, MOBA_BLOCK:])], axis=1)
        key = lax.broadcasted_iota(jnp.int32, s_own.shape, 0)
        qry = lax.broadcasted_iota(jnp.int32, s_own.shape, 1) % MOBA_BLOCK
        s_own = jnp.where(key <= qry, s_own, NEG_BIG)
        s = s_scr[0]
        m8 = jnp.maximum(_column_fold(s_own, jnp.max), _column_fold(s, jnp.max))
        m = jnp.max(m8, axis=0, keepdims=True)
        pb_own = jnp.exp2(s_own - m).astype(BF16)
        acc = jnp.concatenate([_dot(vt_scr[blk_a], pb_own[:, 0:MOBA_BLOCK]),
                               _dot(vt_scr[blk_a + 1], pb_own[:, MOBA_BLOCK:])], axis=1)
        acc = acc + values(0, jnp.exp2(s - m).astype(BF16))
        for c in range(1, k_chunks):
            if c + 1 < k_chunks:
                score(c + 1)
            s = s_scr[c % 2]
            m_new = jnp.maximum(m, jnp.max(_column_fold(s, jnp.max), axis=0, keepdims=True))
            acc = jnp.exp2(m - m_new) * acc + values(c, jnp.exp2(s - m_new).astype(BF16))
            m = m_new
        o_ref[...] = (acc[0:HEAD_DIM, :] * (1.0 / acc[HEAD_DIM:HEAD_DIM + 1, :])).T.astype(o_ref.dtype)

    n_chunks = (blk_a + 1 + KV_CHUNK - 1) // KV_CHUNK
    for k_chunks in range(1, n_blocks // KV_CHUNK + 1):
        pl.when(n_chunks == k_chunks)(functools.partial(attend, k_chunks))


def _prompt_attention(q_all, k_p, v_p, batch, seq):
    n_blocks = seq // MOBA_BLOCK
    assert seq % MOBA_BLOCK == 0 and n_blocks <= LANES
    assert n_blocks % KV_CHUNK == 0 and n_blocks % PREP_BLOCKS == 0
    assert n_blocks % (2 * SUBLANES) == 0
    n_pairs = n_blocks // 2
    pairspec = pl.BlockSpec((2 * MOBA_BLOCK, HEAD_DIM), lambda b, h, i: (b * n_pairs + i, h))
    seqspec = pl.BlockSpec((seq, HEAD_DIM), lambda b, h, i: (b, h))
    return pl.pallas_call(
        functools.partial(_pattn_body, n_blocks),
        grid=(batch, N_HEADS, n_pairs),
        in_specs=[seqspec, seqspec, seqspec],
        out_specs=pairspec,
        out_shape=jax.ShapeDtypeStruct((batch * seq, D_MODEL), BF16),
        scratch_shapes=[pltpu.VMEM((seq, 2 * LANES), BF16),
                        pltpu.VMEM((n_blocks, HEAD_DIM + SUM_ROWS, MOBA_BLOCK), BF16),
                        pltpu.VMEM((n_blocks, 2 * LANES, MOBA_BLOCK), BF16),
                        pltpu.VMEM((n_blocks, HEAD_DIM), F32),
                        pltpu.VMEM((2, KV_CHUNK * MOBA_BLOCK, 2 * MOBA_BLOCK), F32)],
        compiler_params=_params(3),
        name="moba_prompt_attn",
    )(q_all, k_p, v_p)


def _sattn_body(n_pages, dec_seq, pt_ref, q_ref, kn_ref, vn_ref, ck_hbm, cv_hbm, o_ref,
                kbuf, vbuf, sem, s_scr, kmean_scr):
    n = pl.program_id(0)
    n_seq = pl.num_programs(0)
    slot = n % 2
    scale = HEAD_DIM ** -0.5
    pages_per_block = MOBA_BLOCK // PAGE_SIZE
    n_blocks = n_pages // pages_per_block

    def page_copy(hbm, buf, seq, sl, p, kind):
        return pltpu.make_async_copy(hbm.at[pt_ref[seq * n_pages + p]], buf.at[sl, p], sem.at[kind, sl])

    def start_all(seq, sl):
        for p in range(n_pages):
            page_copy(ck_hbm, kbuf, seq, sl, p, 0).start()
            page_copy(cv_hbm, vbuf, seq, sl, p, 1).start()

    @pl.when(n == 0)
    def _():
        start_all(0, 0)

    @pl.when(n + 1 < n_seq)
    def _():
        start_all(n + 1, 1 - slot)

    for p in range(n_pages):
        page_copy(ck_hbm, kbuf, n, slot, p, 0).wait()
        page_copy(cv_hbm, vbuf, n, slot, p, 1).wait()

    def page_rows(buf, p):
        return jnp.concatenate([buf[slot, p, pl.ds(hd, PAGE_SIZE, stride=N_HEADS), :]
                                for hd in range(N_HEADS)], axis=1)

    pad_rows = jnp.zeros((PAGE_SIZE - dec_seq, D_MODEL), F32)
    q_pad = jnp.concatenate([q_ref[0], pad_rows], axis=0)
    row_t = lax.broadcasted_iota(jnp.int32, (PAGE_SIZE, LANES), 0)
    col_c = lax.broadcasted_iota(jnp.int32, (PAGE_SIZE, LANES), 1)
    pick = jnp.where((col_c % dec_seq == row_t) & (col_c < N_HEADS * dec_seq), 1.0, 0.0).astype(BF16)
    head_r = lax.broadcasted_iota(jnp.int32, (D_MODEL, LANES), 0) // HEAD_DIM
    head_c = lax.broadcasted_iota(jnp.int32, (D_MODEL, LANES), 1) // dec_seq
    same_head = head_r == head_c
    q_hi, q_lo = _split_bf16(q_pad)
    qbd_hi = jnp.where(same_head, _dot_t0(q_hi, pick), 0.0).astype(BF16)
    qbd_lo = jnp.where(same_head, _dot_t0(q_lo, pick), 0.0).astype(BF16)

    kmean_scr[...] = jnp.zeros(kmean_scr.shape, F32)
    for b in range(n_blocks):
        ksum = None
        for r in range(pages_per_block):
            p = b * pages_per_block + r
            kp = page_rows(kbuf, p)
            s_scr[p * PAGE_SIZE:(p + 1) * PAGE_SIZE, :] = _dot(kp.astype(BF16), qbd_hi) * scale
            part = jnp.sum(kp, axis=0, keepdims=True)
            ksum = part if ksum is None else ksum + part
        kmean_scr[b:b + 1, :] = ksum * (1.0 / MOBA_BLOCK)
    m_hi, m_lo = _split_bf16(kmean_scr[...])
    gate = _dot(m_hi, qbd_hi) + _dot(m_lo, qbd_hi) + _dot(m_hi, qbd_lo)
    blk = lax.broadcasted_iota(jnp.int32, gate.shape, 0)
    allowed = _top_k_mask(gate, blk < n_blocks, min(MOBA_TOP_K, n_blocks), 0)
    bias = jnp.where(allowed, 0.0, NEG_BIG)

    k_pad = jnp.concatenate([kn_ref[0], pad_rows], axis=0)
    v_pad = jnp.concatenate([vn_ref[0], pad_rows], axis=0)
    s_new = _dot(k_pad.astype(BF16), qbd_hi) * scale
    s_new = jnp.where((row_t <= col_c % dec_seq) & (row_t < dec_seq), s_new, NEG_BIG)

    m = jnp.max(s_new, axis=0, keepdims=True)
    for b in range(n_blocks):
        rows = slice(b * MOBA_BLOCK, (b + 1) * MOBA_BLOCK)
        sb = s_scr[rows, :] + bias[b:b + 1, :]
        s_scr[rows, :] = sb
        m = jnp.maximum(m, jnp.max(sb, axis=0, keepdims=True))
    p_new = jnp.exp(s_new - m)
    l = jnp.sum(p_new, axis=0, keepdims=True)
    for p in range(n_pages):
        rows = slice(p * PAGE_SIZE, (p + 1) * PAGE_SIZE)
        pp = jnp.exp(s_scr[rows, :] - m)
        s_scr[rows, :] = pp
        l = l + jnp.sum(pp, axis=0, keepdims=True)
    inv = 1.0 / l
    out = _dot_t0((p_new * inv).astype(BF16), v_pad.astype(BF16))
    for p in range(n_pages):
        rows = slice(p * PAGE_SIZE, (p + 1) * PAGE_SIZE)
        out = out + _dot_t0((s_scr[rows, :] * inv).astype(BF16), page_rows(vbuf, p).astype(BF16))
    out_c = lax.broadcasted_iota(jnp.int32, out.shape, 0)
    out_h = lax.broadcasted_iota(jnp.int32, out.shape, 1) // HEAD_DIM
    for t in range(dec_seq):
        keep = (out_c % dec_seq == t) & (out_c // dec_seq == out_h)
        o_ref[0, t:t + 1, :] = jnp.sum(jnp.where(keep, out, 0.0), axis=0, keepdims=True)


def _sample_attention(page_table, q_s, k_s, v_s, cache_k, cache_v):
    n_seq, n_pages = page_table.shape
    dec_seq = q_s.shape[1]
    assert n_pages % (MOBA_BLOCK // PAGE_SIZE) == 0 and N_HEADS * dec_seq <= LANES and dec_seq <= SUBLANES
    row_spec = pl.BlockSpec((1, dec_seq, D_MODEL), lambda n, pt: (n, 0, 0))
    any_spec = pl.BlockSpec(memory_space=pl.ANY)
    grid_spec = pltpu.PrefetchScalarGridSpec(
        num_scalar_prefetch=1,
        grid=(n_seq,),
        in_specs=[row_spec, row_spec, row_spec, any_spec, any_spec],
        out_specs=row_spec,
        scratch_shapes=[pltpu.VMEM((2, n_pages, PAGE_SIZE * N_HEADS, HEAD_DIM), F32),
                        pltpu.VMEM((2, n_pages, PAGE_SIZE * N_HEADS, HEAD_DIM), F32),
                        pltpu.SemaphoreType.DMA((2, 2)),
                        pltpu.VMEM((n_pages * PAGE_SIZE, LANES), F32),
                        pltpu.VMEM((2 * SUBLANES, D_MODEL), F32)],
    )
    return pl.pallas_call(
        functools.partial(_sattn_body, n_pages, dec_seq),
        grid_spec=grid_spec,
        out_shape=jax.ShapeDtypeStruct((n_seq, dec_seq, D_MODEL), F32),
        compiler_params=_params(),
        name="moba_sample_attn",
    )(page_table.reshape(-1), q_s, k_s, v_s, cache_k, cache_v)


def _rope_tables(seq, past_len, dec_batch, dec_seq):
    half = ROT_DIM // 2
    pos = np.concatenate([np.arange(seq), past_len + np.repeat(np.arange(dec_seq), dec_batch)]).astype(np.float64)
    inv_freq = ROPE_THETA ** (-np.arange(half, dtype=np.float64) / half)
    ang = pos[:, None] * inv_freq[None, :]
    cos, sin = np.cos(ang), np.sin(ang)
    ones = np.ones((pos.shape[0], LANES - ROT_DIM))
    cos_t = np.concatenate([cos, cos, ones], axis=1).astype(np.float32)
    sin_t = np.concatenate([-sin, sin, 0.0 * ones], axis=1).astype(np.float32)
    return jnp.asarray(cos_t), jnp.asarray(sin_t)


def _to_time_major(a):
    return jnp.swapaxes(a, 0, 1).reshape((a.shape[0] * a.shape[1],) + a.shape[2:])


def _from_time_major(a, dec_batch, dec_seq):
    return jnp.swapaxes(a.reshape((dec_seq, dec_batch) + a.shape[1:]), 0, 1)


def kernel(x_prompt, x_sample, cache_k, cache_v, state_pool, page_table, p_prompt, p_sample, norm_ffn1, norm_mix, norm_ffn2, norm_ple, norm_final, ffn1_w_gate, ffn1_w_up, ffn1_w_down, ffn2_w_gate, ffn2_w_up, ffn2_w_down, ple_w_gate, ple_w_proj, a_w_in, a_ln_g, a_ln_b, a_w_s, a_b_s, a_w_out, b_w_qkv, b_w_out, c_w_in, c_w_grp, c_scale, c_w_out):
    batch, seq, d = x_prompt.shape
    dec_batch, dec_seq, _ = x_sample.shape
    n_prompt = batch * seq
    n_sample = dec_batch * dec_seq
    n_pages = page_table.shape[1]
    past_len = n_pages * PAGE_SIZE
    assert d == D_MODEL and n_sample == ROW_TILE and n_prompt % ROW_TILE == 0
    assert past_len % MOBA_BLOCK == 0

    x = (x_prompt.reshape(n_prompt, d), _to_time_major(x_sample))
    p_rows_prompt = p_prompt.reshape(DEPTH, n_prompt, PLE_DIM)
    p_rows_sample = jnp.swapaxes(p_sample, 1, 2).reshape(DEPTH, n_sample, PLE_DIM)
    rows = lambda v: v.reshape(v.shape[0], 1, v.shape[1])
    bf = lambda w: w.astype(BF16)
    g_ffn1, g_mix, g_ffn2, g_ple = rows(norm_ffn1), rows(norm_mix), rows(norm_ffn2), rows(norm_ple)
    w1 = (bf(ffn1_w_gate), bf(ffn1_w_up), bf(ffn1_w_down))
    w2 = (bf(ffn2_w_gate), bf(ffn2_w_up), bf(ffn2_w_down))
    ple = (g_ple, bf(ple_w_gate), p_rows_prompt, p_rows_sample, bf(ple_w_proj))
    a_w = (bf(a_w_in), rows(a_ln_g), rows(a_ln_b), a_w_s)
    c_w = (bf(c_w_in), bf(c_w_grp), rows(c_scale), bf(c_w_out))
    w_qkv, w_attn_out, a_out = bf(b_w_qkv), bf(b_w_out), bf(a_w_out)

    k_p, v_p, k_s, v_s, pool_p, pool_s, chunk_v = [], [], [], [], [], [], []
    for i in range(DEPTH):
        kind, j = LAYER_MIXERS[i], LAYER_SLOT[i]
        x = _ffn(x, i, g_ffn1, *w1, n_prompt)
        attn = None
        if kind == 0:
            bexp = jnp.repeat(a_b_s[j].T, LANES, axis=1)
            wrow = jnp.repeat(jnp.transpose(a_w_s[j][:, :dec_seq, :dec_seq], (1, 2, 0)), LANES, axis=2)
            brow = jnp.repeat(a_b_s[j][:, :dec_seq].T, LANES, axis=1)
            x, vs = _mixer_a(x, i, j, g_mix, *a_w, bexp, wrow, brow, a_out, n_prompt, dec_batch, dec_seq)
            chunk_v.append(_from_time_major(vs, dec_batch, dec_seq))
        elif kind == 1:
            cos_t, sin_t = _rope_tables(seq, past_len, dec_batch, dec_seq)
            q_all, kp, vp, ks, vs = _qkv(x, i, j, g_mix, w_qkv, cos_t, sin_t, n_prompt, seq)
            o_p = _prompt_attention(q_all, kp, vp, batch, seq)
            q_s = _from_time_major(q_all[n_prompt:], dec_batch, dec_seq)
            ks_n = _from_time_major(ks, dec_batch, dec_seq)
            vs_n = _from_time_major(vs, dec_batch, dec_seq)
            o_s = _sample_attention(page_table, q_s, ks_n, vs_n,
                                    cache_k[j].reshape(-1, PAGE_SIZE * N_HEADS, HEAD_DIM),
                                    cache_v[j].reshape(-1, PAGE_SIZE * N_HEADS, HEAD_DIM))
            attn = (o_p, _to_time_major(o_s).astype(BF16), w_attn_out, j)
            k_p.append(kp.reshape(batch, seq, N_HEADS, HEAD_DIM))
            v_p.append(vp.reshape(batch, seq, N_HEADS, HEAD_DIM))
            k_s.append(ks_n.reshape(dec_batch, dec_seq, N_HEADS, HEAD_DIM))
            v_s.append(vs_n.reshape(dec_batch, dec_seq, N_HEADS, HEAD_DIM))
        else:
            hist_t = jnp.swapaxes(state_pool[j], 0, 1)
            x, tails, us = _mixer_c(x, i, j, g_mix, *c_w, hist_t, n_prompt, seq, dec_batch, dec_seq)
            tps = seq // ROW_TILE
            pool_p.append(tails[tps - 1:batch * tps:tps, HIST_PAD - POOL_HIST:, :])
            u_new = _from_time_major(us, dec_batch, dec_seq)
            pool_s.append(jnp.concatenate([state_pool[j], u_new], axis=1)[:, -POOL_HIST:])
        x = _ffn(x, i, g_ffn2, *w2, n_prompt, attn=attn, ple=ple,
                 final_gain=norm_final.reshape(1, -1) if i + 1 == DEPTH else None)
    y_prompt, y_sample = x
    y_prompt = y_prompt.reshape(batch, seq, d)
    y_sample = _from_time_major(y_sample, dec_batch, dec_seq)
    return (y_prompt, y_sample, jnp.stack(k_p), jnp.stack(v_p), jnp.stack(k_s), jnp.stack(v_s),
            jnp.stack(pool_p), jnp.stack(pool_s), jnp.stack(chunk_v))
```

```python
import functools
import math

import jax
import jax.numpy as jnp
import numpy as np
from jax import lax
from jax.experimental import pallas as pl
from jax.experimental.pallas import tpu as pltpu

F32 = jnp.float32
BF16 = jnp.bfloat16

D_MODEL = 1024
DEPTH = 4
D_FF = 2816
PLE_DIM = 256
RMS_EPS = 1e-6
LN_EPS = 1e-5
CHUNK = 128
A_GROUPS = 8
N_HEADS = 8
HEAD_DIM = 128
ROT_DIM = 32
ROPE_THETA = 500000.0
MOBA_BLOCK = 256
MOBA_TOP_K = 3
PAGE_SIZE = 128
POOL_WINDOWS = (2, 4, 8, 16)
C_GROUP_DIM = 256
POOL_HIST = 15
LAYER_MIXERS = (0, 1, 2, 0)
LAYER_SLOT = (0, 0, 0, 1)

LANES = 128
SUBLANES = 8
VMEM_LIMIT = 56 * 1024 * 1024

ROW_TILE = 512
MIX_ROW_GROUPS = 2
NEG_BIG = -1e30


def _dot(a, b):
    return jnp.dot(a, b, preferred_element_type=F32)


def _dot_t0(a, b):
    return lax.dot_general(a, b, (((0,), (0,)), ((), ())), preferred_element_type=F32)


def _dot_t1(a, b):
    return lax.dot_general(a, b, (((1,), (1,)), ((), ())), preferred_element_type=F32)


def _rms(x, g):
    return x * lax.rsqrt(jnp.mean(x * x, axis=-1, keepdims=True) + RMS_EPS) * g


def _split_bf16(x):
    hi = x.astype(BF16)
    lo = (x - hi.astype(F32)).astype(BF16)
    return hi, lo


def _const_spec(shape):
    nd = len(shape)
    return pl.BlockSpec(shape, lambda *_: (0,) * nd, pipeline_mode=pl.Buffered(1))


def _layer_spec(shape, layer):
    nd = len(shape)
    return pl.BlockSpec((None,) + tuple(shape), lambda *_: (layer,) + (0,) * nd, pipeline_mode=pl.Buffered(1))


def _row_spec(width):
    return pl.BlockSpec((ROW_TILE, width), lambda i: (i, 0))


def _params(n_axes=1):
    return pltpu.CompilerParams(dimension_semantics=("arbitrary",) * n_axes,
                                vmem_limit_bytes=VMEM_LIMIT)


def _ffn_body(n_prompt_tiles, split_x, attn, ple, final, cast, *refs):
    refs = list(refs)
    is_prompt = pl.program_id(0) < n_prompt_tiles
    take = lambda n: [refs.pop(0) for _ in range(n)]
    x_refs = take(2 if split_x else 1)
    attn_refs = take(3 if attn else 0)
    g_ref, wg_ref, wu_ref, wd_ref = take(4)
    ple_refs = take(5 if ple else 0)
    final_refs = take(1 if final else 0)
    cast_in = take(3 if cast else 0)
    out_refs = take(2 if final else 1)
    cast_out = refs

    for src, dst in zip(cast_in, cast_out):
        dst[...] = src[...].astype(BF16)

    def pick(prompt_ref, sample_ref):
        return jnp.where(is_prompt, prompt_ref[...], sample_ref[...])

    x = pick(*x_refs) if split_x else x_refs[0][...]
    if attn:
        op_ref, os_ref, wo_ref = attn_refs
        x = x + _dot(pick(op_ref, os_ref), wo_ref[...])
    h = _rms(x, g_ref[...]).astype(BF16)
    gate = _dot(h, wg_ref[...])
    up = _dot(h, wu_ref[...])
    a = (gate * jax.nn.sigmoid(gate) * up).astype(BF16)
    x = x + 0.5 * _dot(a, wd_ref[...])
    if ple:
        gp_ref, wgate_ref, pp_ref, ps_ref, wproj_ref = ple_refs
        h = _rms(x, gp_ref[...]).astype(BF16)
        x = x + jax.nn.sigmoid(_dot(h, wgate_ref[...])) * _dot(pick(pp_ref, ps_ref).astype(BF16), wproj_ref[...])
    if not final:
        out_refs[0][...] = x
        return
    y = _rms(x, final_refs[0][...])
    yp_ref, ys_ref = out_refs

    @pl.when(is_prompt)
    def _():
        yp_ref[...] = y

    @pl.when(jnp.logical_not(is_prompt))
    def _():
        ys_ref[...] = y


WEIGHT_CAST_CHUNKS = 16


def _ffn(x, layer, gains, weights, n_prompt, attn=None, ple=None, final_gain=None, cast_next=None):
    wg, wu, wd = weights
    npt = n_prompt // ROW_TILE
    prompt_rows = lambda i: (jnp.minimum(i, npt - 1), 0)
    sample_rows = lambda i: (0, 0)
    split_x = isinstance(x, tuple)
    if split_x:
        args = list(x)
        in_specs = [pl.BlockSpec((ROW_TILE, D_MODEL), prompt_rows), pl.BlockSpec((ROW_TILE, D_MODEL), sample_rows)]
        n = n_prompt + x[1].shape[0]
    else:
        args = [x]
        in_specs = [_row_spec(D_MODEL)]
        n = x.shape[0]
    assert n - n_prompt == ROW_TILE
    if attn is not None:
        o_p, o_s, w_out, slot = attn
        args += [o_p, o_s, w_out]
        in_specs += [pl.BlockSpec((ROW_TILE, D_MODEL), prompt_rows), pl.BlockSpec((ROW_TILE, D_MODEL), sample_rows),
                     _layer_spec((D_MODEL, D_MODEL), slot)]
    args += [gains, wg, wu, wd]
    in_specs += [_layer_spec((1, D_MODEL), layer), _const_spec((D_MODEL, D_FF)),
                 _const_spec((D_MODEL, D_FF)), _const_spec((D_FF, D_MODEL))]
    if ple is not None:
        gp, wgate, p_prompt, p_sample, wproj = ple
        args += [gp, wgate, p_prompt, p_sample, wproj]
        in_specs += [_layer_spec((1, D_MODEL), layer), _layer_spec((D_MODEL, D_MODEL), layer),
                     pl.BlockSpec((None, ROW_TILE, PLE_DIM), lambda i: (layer, jnp.minimum(i, npt - 1), 0)),
                     pl.BlockSpec((None, ROW_TILE, PLE_DIM), lambda i: (layer, 0, 0)),
                     _layer_spec((PLE_DIM, D_MODEL), layer)]
    if final_gain is None:
        out_specs = [_row_spec(D_MODEL)]
        out_shape = [jax.ShapeDtypeStruct((n, D_MODEL), F32)]
    else:
        args.append(final_gain)
        in_specs.append(_const_spec((1, D_MODEL)))
        out_specs = [pl.BlockSpec((ROW_TILE, D_MODEL), prompt_rows), pl.BlockSpec((ROW_TILE, D_MODEL), sample_rows)]
        out_shape = [jax.ShapeDtypeStruct((n_prompt, D_MODEL), F32), jax.ShapeDtypeStruct((ROW_TILE, D_MODEL), F32)]
    n_main = len(out_shape)
    if cast_next is not None:
        *stacks, next_layer = cast_next
        assert n // ROW_TILE >= WEIGHT_CAST_CHUNKS
        for w in stacks:
            rows, cols = w.shape[1] // WEIGHT_CAST_CHUNKS, w.shape[2]
            assert rows * WEIGHT_CAST_CHUNKS == w.shape[1] and rows % (2 * SUBLANES) == 0
            chunk = lambda i: jnp.minimum(i, WEIGHT_CAST_CHUNKS - 1)
            args.append(w)
            in_specs.append(pl.BlockSpec((None, rows, cols), lambda i, chunk=chunk: (next_layer, chunk(i), 0)))
            out_specs.append(pl.BlockSpec((rows, cols), lambda i, chunk=chunk: (chunk(i), 0)))
            out_shape.append(jax.ShapeDtypeStruct(w.shape[1:], BF16))
    outs = pl.pallas_call(
        functools.partial(_ffn_body, npt, split_x, attn is not None, ple is not None, final_gain is not None,
                          cast_next is not None),
        grid=(n // ROW_TILE,),
        in_specs=in_specs,
        out_specs=out_specs,
        out_shape=out_shape,
        compiler_params=_params(),
        name="ffn_ple" if ple is not None else "ffn",
    )(*args)
    main = outs[0] if n_main == 1 else tuple(outs[:n_main])
    return main, tuple(outs[n_main:])


def _mixa_body(n_prompt_tiles, dec_batch, dec_seq,
               x_ref, g_ref, win_ref, lng_ref, lnb_ref, ws_ref, bexp_ref, wrow_ref, brow_ref, wout_ref,
               o_ref, vs_ref, v_scr, m_scr):
    i = pl.program_id(0)

    def project(rows):
        return _dot(_rms(x_ref[rows, :], g_ref[...]).astype(BF16), win_ref[...])

    def activate(rows, z):
        z = 0.5 * z * (1.0 + lax.erf(z * (1.0 / math.sqrt(2.0))))
        v = z[:, D_MODEL:]
        mu = jnp.mean(v, axis=-1, keepdims=True)
        vc = v - mu
        var = jnp.mean(vc * vc, axis=-1, keepdims=True)
        v_scr[rows, :] = vc * lax.rsqrt(var + LN_EPS) * lng_ref[...] + lnb_ref[...]
        return z[:, :D_MODEL]

    def back(rows, u):
        o_ref[rows, :] = x_ref[rows, :] + _dot((u * m_scr[rows, :]).astype(BF16), wout_ref[...])

    @pl.when(i < n_prompt_tiles)
    def _():
        r_iota = lax.broadcasted_iota(jnp.int32, (CHUNK, CHUNK), 0)
        c_iota = lax.broadcasted_iota(jnp.int32, (CHUNK, CHUNK), 1)
        w_tril = [jnp.where(c_iota <= r_iota, ws_ref[g], 0.0).astype(BF16) for g in range(A_GROUPS)]
        group_rows = ROW_TILE // MIX_ROW_GROUPS
        groups = [slice(r * group_rows, (r + 1) * group_rows) for r in range(MIX_ROW_GROUPS)]

        def mix(r):
            chunks = range(r * group_rows // CHUNK, (r + 1) * group_rows // CHUNK)
            for g in range(A_GROUPS):
                gs = slice(g * LANES, (g + 1) * LANES)
                rhs = jnp.concatenate([v_scr[c * CHUNK:(c + 1) * CHUNK, gs] for c in chunks], axis=1).astype(BF16)
                mg = _dot(w_tril[g], rhs)
                for k, c in enumerate(chunks):
                    m_scr[c * CHUNK:(c + 1) * CHUNK, gs] = mg[:, k * LANES:(k + 1) * LANES] + bexp_ref[:, gs]

        z = project(groups[0])
        for r in range(MIX_ROW_GROUPS):
            z_next = project(groups[r + 1]) if r + 1 < MIX_ROW_GROUPS else None
            u = activate(groups[r], z)
            mix(r)
            back(groups[r], u)
            z = z_next

    @pl.when(i >= n_prompt_tiles)
    def _():
        rows = slice(0, ROW_TILE)
        u = activate(rows, project(rows))
        for t in range(dec_seq):
            acc = brow_ref[t:t + 1, :] + wrow_ref[t, 0:1, :] * v_scr[0:dec_batch, :]
            for s in range(1, t + 1):
                acc = acc + wrow_ref[t, s:s + 1, :] * v_scr[s * dec_batch:(s + 1) * dec_batch, :]
            m_scr[t * dec_batch:(t + 1) * dec_batch, :] = acc
        vs_ref[...] = v_scr[...]
        back(rows, u)


def _mixer_a(x, layer, slot, g, w_in, ln_g, ln_b, w_s, bexp, wrow, brow, w_out, n_prompt, dec_batch, dec_seq):
    n = x.shape[0]
    assert n - n_prompt == ROW_TILE == dec_batch * dec_seq and dec_batch % SUBLANES == 0
    body = functools.partial(_mixa_body, n_prompt // ROW_TILE, dec_batch, dec_seq)
    return pl.pallas_call(
        body,
        grid=(n // ROW_TILE,),
        in_specs=[_row_spec(D_MODEL), _layer_spec((1, D_MODEL), layer), _layer_spec((D_MODEL, 2 * D_MODEL), slot),
                  _layer_spec((1, D_MODEL), slot), _layer_spec((1, D_MODEL), slot),
                  _layer_spec((A_GROUPS, CHUNK, CHUNK), slot), _const_spec((CHUNK, D_MODEL)),
                  _const_spec((dec_seq, dec_seq, D_MODEL)), _const_spec((dec_seq, D_MODEL)),
                  _layer_spec((D_MODEL, D_MODEL), slot)],
        out_specs=[_row_spec(D_MODEL), pl.BlockSpec((ROW_TILE, D_MODEL), lambda i: (0, 0))],
        out_shape=[jax.ShapeDtypeStruct((n, D_MODEL), F32),
                   jax.ShapeDtypeStruct((ROW_TILE, D_MODEL), F32)],
        scratch_shapes=[pltpu.VMEM((ROW_TILE, D_MODEL), F32), pltpu.VMEM((ROW_TILE, D_MODEL), F32)],
        compiler_params=_params(),
        name="mixer_a",
    )(x, g, w_in, ln_g, ln_b, w_s, bexp, wrow, brow, w_out)


HIST_PAD = 16


def _mixc_body(n_prompt_tiles, tiles_per_seq, dec_batch, dec_seq,
               x_ref, g_ref, win_ref, wgrp_ref, scale_ref, wout_ref, hist_ref,
               o_ref, tail_ref, us_ref, ext_scr, pool_scr):
    i = pl.program_id(0)

    def project(rows):
        return _dot(_rms(x_ref[rows, :], g_ref[...]).astype(BF16), win_ref[...])

    def finish(rows):
        parts = []
        for g in range(len(POOL_WINDOWS)):
            gs = slice(g * C_GROUP_DIM, (g + 1) * C_GROUP_DIM)
            parts.append(_dot(pool_scr[rows, gs].astype(BF16), wgrp_ref[g]))
        mixed = jnp.concatenate(parts, axis=1) * scale_ref[...]
        o_ref[rows, :] = x_ref[rows, :] + _dot(mixed.astype(BF16), wout_ref[...])

    @pl.when(i < n_prompt_tiles)
    def _():
        @pl.when(i % tiles_per_seq == 0)
        def _():
            ext_scr[0:HIST_PAD, :] = jnp.zeros((HIST_PAD, D_MODEL), F32)

        group_rows = ROW_TILE // MIX_ROW_GROUPS

        def pool(r):
            lo = r * group_rows
            pos = (i % tiles_per_seq) * ROW_TILE + lo + lax.broadcasted_iota(jnp.int32, (group_rows, 1), 0)
            for g, win in enumerate(POOL_WINDOWS):
                gs = slice(g * C_GROUP_DIM, (g + 1) * C_GROUP_DIM)
                cur = ext_scr[HIST_PAD + lo:HIST_PAD + lo + group_rows, gs]
                tot = cur
                for d in range(1, win):
                    tot = tot + ext_scr[HIST_PAD + lo - d:HIST_PAD + lo - d + group_rows, gs]
                inv = 1.0 / jnp.minimum(pos + 1, win).astype(F32)
                pool_scr[lo:lo + group_rows, gs] = tot * inv - cur

        for r in range(MIX_ROW_GROUPS):
            if r == 0:
                ext_scr[HIST_PAD:HIST_PAD + group_rows, :] = project(slice(0, group_rows))
            if r + 1 < MIX_ROW_GROUPS:
                nxt = slice((r + 1) * group_rows, (r + 2) * group_rows)
                ext_scr[HIST_PAD + nxt.start:HIST_PAD + nxt.stop, :] = project(nxt)
            pool(r)
            finish(slice(r * group_rows, (r + 1) * group_rows))
        tail = ext_scr[ROW_TILE:, :]
        tail_ref[0] = tail
        ext_scr[0:HIST_PAD, :] = tail

    @pl.when(i >= n_prompt_tiles)
    def _():
        u = project(slice(0, ROW_TILE))
        ext_scr[0:ROW_TILE, :] = u
        us_ref[...] = u
        tail_ref[0] = u[ROW_TILE - HIST_PAD:, :]

        def ext_row(j, gs):
            if j < POOL_HIST:
                return hist_ref[j, :, gs]
            return ext_scr[(j - POOL_HIST) * dec_batch:(j - POOL_HIST + 1) * dec_batch, gs]

        for t in range(dec_seq):
            for g, win in enumerate(POOL_WINDOWS):
                gs = slice(g * C_GROUP_DIM, (g + 1) * C_GROUP_DIM)
                cur = ext_row(POOL_HIST + t, gs)
                tot = cur
                for d in range(1, win):
                    tot = tot + ext_row(POOL_HIST + t - d, gs)
                pool_scr[t * dec_batch:(t + 1) * dec_batch, gs] = tot * (1.0 / win) - cur
        finish(slice(0, ROW_TILE))


def _mixer_c(x, layer, slot, g, w_in, w_grp, scale, w_out, hist_t, n_prompt, seq, dec_batch, dec_seq):
    n = x.shape[0]
    nt = n // ROW_TILE
    assert n - n_prompt == ROW_TILE == dec_batch * dec_seq and seq % ROW_TILE == 0
    assert POOL_HIST <= HIST_PAD and max(POOL_WINDOWS) - 1 <= POOL_HIST
    body = functools.partial(_mixc_body, n_prompt // ROW_TILE, seq // ROW_TILE, dec_batch, dec_seq)
    return pl.pallas_call(
        body,
        grid=(nt,),
        in_specs=[_row_spec(D_MODEL), _layer_spec((1, D_MODEL), layer), _layer_spec((D_MODEL, D_MODEL), slot),
                  _layer_spec((len(POOL_WINDOWS), C_GROUP_DIM, C_GROUP_DIM), slot), _layer_spec((1, D_MODEL), slot),
                  _layer_spec((D_MODEL, D_MODEL), slot), _const_spec((POOL_HIST, dec_batch, D_MODEL))],
        out_specs=[_row_spec(D_MODEL),
                   pl.BlockSpec((1, HIST_PAD, D_MODEL), lambda i: (i, 0, 0)),
                   pl.BlockSpec((ROW_TILE, D_MODEL), lambda i: (0, 0))],
        out_shape=[jax.ShapeDtypeStruct((n, D_MODEL), F32),
                   jax.ShapeDtypeStruct((nt, HIST_PAD, D_MODEL), F32),
                   jax.ShapeDtypeStruct((ROW_TILE, D_MODEL), F32)],
        scratch_shapes=[pltpu.VMEM((HIST_PAD + ROW_TILE, D_MODEL), F32),
                        pltpu.VMEM((ROW_TILE, D_MODEL), F32)],
        compiler_params=_params(),
        name="mixer_c",
    )(x, g, w_in, w_grp, scale, w_out, hist_t)


def _rope(xh, cos_t, sin_t, lane):
    partner = jnp.where(lane < ROT_DIM // 2, pltpu.roll(xh, LANES - ROT_DIM // 2, 1),
                        pltpu.roll(xh, ROT_DIM // 2, 1))
    return xh * cos_t + partner * sin_t


def _qkv_body(n_prompt_tiles, x_ref, g_ref, w_ref, cos_ref, sin_ref,
              q_ref, kp_ref, vp_ref, ks_ref, vs_ref):
    i = pl.program_id(0)
    group_rows = ROW_TILE // MIX_ROW_GROUPS
    groups = [slice(r * group_rows, (r + 1) * group_rows) for r in range(MIX_ROW_GROUPS)]
    lane = lax.broadcasted_iota(jnp.int32, (group_rows, LANES), 1)

    def project(rows):
        return _dot(_rms(x_ref[rows, :], g_ref[...]).astype(BF16), w_ref[...])

    def emit(k_ref, v_ref):
        def rotate(rows, qkv):
            cos_t = cos_ref[rows, :]
            sin_t = sin_ref[rows, :]
            for hd in range(N_HEADS):
                cols = slice(hd * HEAD_DIM, (hd + 1) * HEAD_DIM)
                q_ref[rows, cols] = _rope(qkv[:, cols], cos_t, sin_t, lane)
                off = D_MODEL + hd * HEAD_DIM
                k_ref[rows, cols] = _rope(qkv[:, off:off + HEAD_DIM], cos_t, sin_t, lane)
            v_ref[rows, :] = qkv[:, 2 * D_MODEL:]

        qkv = project(groups[0])
        for r in range(MIX_ROW_GROUPS):
            qkv_next = project(groups[r + 1]) if r + 1 < MIX_ROW_GROUPS else None
            rotate(groups[r], qkv)
            qkv = qkv_next

    pl.when(i < n_prompt_tiles)(functools.partial(emit, kp_ref, vp_ref))
    pl.when(i >= n_prompt_tiles)(functools.partial(emit, ks_ref, vs_ref))


def _qkv(x, layer, slot, g, w_qkv, cos_t, sin_t, n_prompt, seq):
    n = x.shape[0]
    npt = n_prompt // ROW_TILE
    tps = seq // ROW_TILE
    assert n - n_prompt == ROW_TILE and cos_t.shape[0] == seq + ROW_TILE
    prompt_spec = pl.BlockSpec((ROW_TILE, D_MODEL), lambda i: (jnp.minimum(i, npt - 1), 0))
    sample_spec = pl.BlockSpec((ROW_TILE, D_MODEL), lambda i: (0, 0))
    rope_spec = pl.BlockSpec((ROW_TILE, LANES), lambda i: (jnp.where(i < npt, i % tps, tps), 0))
    return pl.pallas_call(
        functools.partial(_qkv_body, npt),
        grid=(n // ROW_TILE,),
        in_specs=[_row_spec(D_MODEL), _layer_spec((1, D_MODEL), layer), _layer_spec((D_MODEL, 3 * D_MODEL), slot),
                  rope_spec, rope_spec],
        out_specs=[_row_spec(D_MODEL), prompt_spec, prompt_spec, sample_spec, sample_spec],
        out_shape=[jax.ShapeDtypeStruct((n, D_MODEL), F32),
                   jax.ShapeDtypeStruct((n_prompt, D_MODEL), F32),
                   jax.ShapeDtypeStruct((n_prompt, D_MODEL), F32),
                   jax.ShapeDtypeStruct((ROW_TILE, D_MODEL), F32),
                   jax.ShapeDtypeStruct((ROW_TILE, D_MODEL), F32)],
        compiler_params=_params(),
        name="moba_qkv",
    )(x, g, w_qkv, cos_t, sin_t)


def _top_k_mask(gate, valid, k, axis):
    idx = lax.broadcasted_iota(jnp.int32, gate.shape, axis)
    size = gate.shape[axis]
    g = jnp.where(valid, gate, -jnp.inf)
    chosen = jnp.zeros(gate.shape, dtype=jnp.bool_)
    for _ in range(k):
        mx = jnp.max(g, axis=axis, keepdims=True)
        first = jnp.min(jnp.where(g == mx, idx, size), axis=axis, keepdims=True)
        pick = (idx == first) & (mx > -jnp.inf)
        chosen = chosen | pick
        g = jnp.where(pick, -jnp.inf, g)
    return chosen


KV_CHUNK = 4
PREP_BLOCKS = 4
SUM_ROWS = 16


def _column_fold(x, op):
    return op(x.reshape(x.shape[0] // SUBLANES, SUBLANES, x.shape[1]), axis=0)


def _pattn_body(n_blocks, q_ref, k_ref, v_ref, o_ref, kaug_scr, vt_scr, qaug_scr, kmean_scr, s_scr):
    qp = pl.program_id(2)
    exp2_scale = (HEAD_DIM ** -0.5) * math.log2(math.e)
    chunk = KV_CHUNK * MOBA_BLOCK

    @pl.when(qp == 0)
    def _():
        lane = lax.broadcasted_iota(jnp.int32, (MOBA_BLOCK, LANES), 1)
        sum_rows = jnp.where(lax.broadcasted_iota(jnp.int32, (SUM_ROWS, MOBA_BLOCK), 0) == 0, 1.0, 0.0).astype(BF16)

        def fill(j, carry):
            rows = pl.ds(pl.multiple_of(j * MOBA_BLOCK, MOBA_BLOCK), MOBA_BLOCK)
            kj = k_ref[rows, :]
            kaug_scr[rows, 0:HEAD_DIM] = kj.astype(BF16)
            kaug_scr[rows, HEAD_DIM:] = jnp.where(lane == j, 1.0, 0.0).astype(BF16)
            vt_scr[j] = jnp.concatenate([v_ref[rows, :].T.astype(BF16), sum_rows], axis=0)
            kmean_scr[pl.ds(j, 1), :] = jnp.mean(kj, axis=0, keepdims=True)
            return carry

        lax.fori_loop(0, n_blocks, fill, 0, unroll=2)
        m_hi, m_lo = _split_bf16(kmean_scr[...])
        width = PREP_BLOCKS * MOBA_BLOCK
        blk = lax.broadcasted_iota(jnp.int32, (n_blocks, width), 0)
        col_blk = lax.broadcasted_iota(jnp.int32, (n_blocks, width), 1) // MOBA_BLOCK
        pad = jnp.zeros((LANES - n_blocks, MOBA_BLOCK), BF16)

        def prep(jj, carry):
            rows = pl.ds(pl.multiple_of(jj * width, width), width)
            qt = q_ref[rows, :].T
            qt_hi, qt_lo = _split_bf16(qt)
            gate = _dot(m_hi, qt_hi) + _dot(m_lo, qt_hi) + _dot(m_hi, qt_lo)
            allowed = _top_k_mask(gate, blk < col_blk + jj * PREP_BLOCKS, MOBA_TOP_K, 0)
            bias = jnp.where(allowed, 0.0, NEG_BIG).astype(BF16)
            qt_op = (qt * exp2_scale).astype(BF16)
            for i in range(PREP_BLOCKS):
                cols = slice(i * MOBA_BLOCK, (i + 1) * MOBA_BLOCK)
                qaug_scr[jj * PREP_BLOCKS + i] = jnp.concatenate([qt_op[:, cols], bias[:, cols], pad], axis=0)
            return carry

        lax.fori_loop(0, n_blocks // PREP_BLOCKS, prep, 0, unroll=2)

    blk_a = 2 * qp
    q_aug = jnp.concatenate([qaug_scr[blk_a], qaug_scr[blk_a + 1]], axis=1)
    own_a = pl.ds(pl.multiple_of(blk_a * MOBA_BLOCK, MOBA_BLOCK), MOBA_BLOCK)
    own_b = pl.ds(pl.multiple_of((blk_a + 1) * MOBA_BLOCK, MOBA_BLOCK), MOBA_BLOCK)

    def attend(k_chunks):
        def score(c):
            s_scr[c % 2] = _dot(kaug_scr[c * chunk:(c + 1) * chunk, :], q_aug)

        def values(c, pb):
            pv = _dot(vt_scr[c * KV_CHUNK], pb[0:MOBA_BLOCK, :])
            for j in range(1, KV_CHUNK):
                pv = pv + _dot(vt_scr[c * KV_CHUNK + j], pb[j * MOBA_BLOCK:(j + 1) * MOBA_BLOCK, :])
            return pv

        score(0)
        if k_chunks > 1:
            score(1)
        s_own = jnp.concatenate(
            [_dot(kaug_scr[own_a, 0:HEAD_DIM], q_aug[0:HEAD_DIM, 0:MOBA_BLOCK]),
             _dot(kaug_scr[own_b, 0:HEAD_DIM], q_aug[0:HEAD_DIM, MOBA_BLOCK:])], axis=1)
        key = lax.broadcasted_iota(jnp.int32, s_own.shape, 0)
        qry = lax.broadcasted_iota(jnp.int32, s_own.shape, 1) % MOBA_BLOCK
        s_own = jnp.where(key <= qry, s_own, NEG_BIG)
        s = s_scr[0]
        m8 = jnp.maximum(_column_fold(s_own, jnp.max), _column_fold(s, jnp.max))
        m = jnp.max(m8, axis=0, keepdims=True)
        pb_own = jnp.exp2(s_own - m).astype(BF16)
        acc = jnp.concatenate([_dot(vt_scr[blk_a], pb_own[:, 0:MOBA_BLOCK]),
                               _dot(vt_scr[blk_a + 1], pb_own[:, MOBA_BLOCK:])], axis=1)
        acc = acc + values(0, jnp.exp2(s - m).astype(BF16))
        for c in range(1, k_chunks):
            if c + 1 < k_chunks:
                score(c + 1)
            s = s_scr[c % 2]
            m_new = jnp.maximum(m, jnp.max(_column_fold(s, jnp.max), axis=0, keepdims=True))
            acc = jnp.exp2(m - m_new) * acc + values(c, jnp.exp2(s - m_new).astype(BF16))
            m = m_new
        o_ref[...] = (acc[0:HEAD_DIM, :] * (1.0 / acc[HEAD_DIM:HEAD_DIM + 1, :])).T.astype(o_ref.dtype)

    n_chunks = (blk_a + 1 + KV_CHUNK - 1) // KV_CHUNK
    for k_chunks in range(1, n_blocks // KV_CHUNK + 1):
        pl.when(n_chunks == k_chunks)(functools.partial(attend, k_chunks))


def _prompt_attention(q_all, k_p, v_p, batch, seq):
    n_blocks = seq // MOBA_BLOCK
    assert seq % MOBA_BLOCK == 0 and n_blocks <= LANES
    assert n_blocks % KV_CHUNK == 0 and n_blocks % PREP_BLOCKS == 0
    assert n_blocks % (2 * SUBLANES) == 0
    n_pairs = n_blocks // 2
    pairspec = pl.BlockSpec((2 * MOBA_BLOCK, HEAD_DIM), lambda b, h, i: (b * n_pairs + i, h))
    seqspec = pl.BlockSpec((seq, HEAD_DIM), lambda b, h, i: (b, h))
    return pl.pallas_call(
        functools.partial(_pattn_body, n_blocks),
        grid=(batch, N_HEADS, n_pairs),
        in_specs=[seqspec, seqspec, seqspec],
        out_specs=pairspec,
        out_shape=jax.ShapeDtypeStruct((batch * seq, D_MODEL), BF16),
        scratch_shapes=[pltpu.VMEM((seq, 2 * LANES), BF16),
                        pltpu.VMEM((n_blocks, HEAD_DIM + SUM_ROWS, MOBA_BLOCK), BF16),
                        pltpu.VMEM((n_blocks, 2 * LANES, MOBA_BLOCK), BF16),
                        pltpu.VMEM((n_blocks, HEAD_DIM), F32),
                        pltpu.VMEM((2, KV_CHUNK * MOBA_BLOCK, 2 * MOBA_BLOCK), F32)],
        compiler_params=_params(3),
        name="moba_prompt_attn",
    )(q_all, k_p, v_p)


def _sattn_body(n_pages, dec_seq, pt_ref, q_ref, kn_ref, vn_ref, ck_hbm, cv_hbm, o_ref,
                kbuf, vbuf, sem, s_scr, kmean_scr):
    n = pl.program_id(0)
    n_seq = pl.num_programs(0)
    slot = n % 2
    scale = HEAD_DIM ** -0.5
    pages_per_block = MOBA_BLOCK // PAGE_SIZE
    n_blocks = n_pages // pages_per_block

    def page_copy(hbm, buf, seq, sl, p, kind):
        return pltpu.make_async_copy(hbm.at[pt_ref[seq * n_pages + p]], buf.at[sl, p], sem.at[kind, sl])

    def start_all(seq, sl):
        for p in range(n_pages):
            page_copy(ck_hbm, kbuf, seq, sl, p, 0).start()
            page_copy(cv_hbm, vbuf, seq, sl, p, 1).start()

    @pl.when(n == 0)
    def _():
        start_all(0, 0)

    @pl.when(n + 1 < n_seq)
    def _():
        start_all(n + 1, 1 - slot)

    for p in range(n_pages):
        page_copy(ck_hbm, kbuf, n, slot, p, 0).wait()
        page_copy(cv_hbm, vbuf, n, slot, p, 1).wait()

    def page_rows(buf, p):
        return jnp.concatenate([buf[slot, p, pl.ds(hd, PAGE_SIZE, stride=N_HEADS), :]
                                for hd in range(N_HEADS)], axis=1)

    pad_rows = jnp.zeros((PAGE_SIZE - dec_seq, D_MODEL), F32)
    q_pad = jnp.concatenate([q_ref[0], pad_rows], axis=0)
    row_t = lax.broadcasted_iota(jnp.int32, (PAGE_SIZE, LANES), 0)
    col_c = lax.broadcasted_iota(jnp.int32, (PAGE_SIZE, LANES), 1)
    pick = jnp.where((col_c % dec_seq == row_t) & (col_c < N_HEADS * dec_seq), 1.0, 0.0).astype(BF16)
    head_r = lax.broadcasted_iota(jnp.int32, (D_MODEL, LANES), 0) // HEAD_DIM
    head_c = lax.broadcasted_iota(jnp.int32, (D_MODEL, LANES), 1) // dec_seq
    same_head = head_r == head_c
    q_hi, q_lo = _split_bf16(q_pad)
    qbd_hi = jnp.where(same_head, _dot_t0(q_hi, pick), 0.0).astype(BF16)
    qbd_lo = jnp.where(same_head, _dot_t0(q_lo, pick), 0.0).astype(BF16)

    kmean_scr[...] = jnp.zeros(kmean_scr.shape, F32)
    for b in range(n_blocks):
        ksum = None
        for r in range(pages_per_block):
            p = b * pages_per_block + r
            kp = page_rows(kbuf, p)
            s_scr[p * PAGE_SIZE:(p + 1) * PAGE_SIZE, :] = _dot(kp.astype(BF16), qbd_hi) * scale
            part = jnp.sum(kp, axis=0, keepdims=True)
            ksum = part if ksum is None else ksum + part
        kmean_scr[b:b + 1, :] = ksum * (1.0 / MOBA_BLOCK)
    m_hi, m_lo = _split_bf16(kmean_scr[...])
    gate = _dot(m_hi, qbd_hi) + _dot(m_lo, qbd_hi) + _dot(m_hi, qbd_lo)
    blk = lax.broadcasted_iota(jnp.int32, gate.shape, 0)
    allowed = _top_k_mask(gate, blk < n_blocks, min(MOBA_TOP_K, n_blocks), 0)
    bias = jnp.where(allowed, 0.0, NEG_BIG)

    k_pad = jnp.concatenate([kn_ref[0], pad_rows], axis=0)
    v_pad = jnp.concatenate([vn_ref[0], pad_rows], axis=0)
    s_new = _dot(k_pad.astype(BF16), qbd_hi) * scale
    s_new = jnp.where((row_t <= col_c % dec_seq) & (row_t < dec_seq), s_new, NEG_BIG)

    m = jnp.max(s_new, axis=0, keepdims=True)
    for b in range(n_blocks):
        rows = slice(b * MOBA_BLOCK, (b + 1) * MOBA_BLOCK)
        sb = s_scr[rows, :] + bias[b:b + 1, :]
        s_scr[rows, :] = sb
        m = jnp.maximum(m, jnp.max(sb, axis=0, keepdims=True))
    p_new = jnp.exp(s_new - m)
    l = jnp.sum(p_new, axis=0, keepdims=True)
    for p in range(n_pages):
        rows = slice(p * PAGE_SIZE, (p + 1) * PAGE_SIZE)
        pp = jnp.exp(s_scr[rows, :] - m)
        s_scr[rows, :] = pp
        l = l + jnp.sum(pp, axis=0, keepdims=True)
    inv = 1.0 / l
    out = _dot_t0((p_new * inv).astype(BF16), v_pad.astype(BF16))
    for p in range(n_pages):
        rows = slice(p * PAGE_SIZE, (p + 1) * PAGE_SIZE)
        out = out + _dot_t0((s_scr[rows, :] * inv).astype(BF16), page_rows(vbuf, p).astype(BF16))
    out_c = lax.broadcasted_iota(jnp.int32, out.shape, 0)
    out_h = lax.broadcasted_iota(jnp.int32, out.shape, 1) // HEAD_DIM
    for t in range(dec_seq):
        keep = (out_c % dec_seq == t) & (out_c // dec_seq == out_h)
        o_ref[0, t:t + 1, :] = jnp.sum(jnp.where(keep, out, 0.0), axis=0, keepdims=True)


def _sample_attention(page_table, q_s, k_s, v_s, cache_k, cache_v):
    n_seq, n_pages = page_table.shape
    dec_seq = q_s.shape[1]
    assert n_pages % (MOBA_BLOCK // PAGE_SIZE) == 0 and N_HEADS * dec_seq <= LANES and dec_seq <= SUBLANES
    row_spec = pl.BlockSpec((1, dec_seq, D_MODEL), lambda n, pt: (n, 0, 0))
    any_spec = pl.BlockSpec(memory_space=pl.ANY)
    grid_spec = pltpu.PrefetchScalarGridSpec(
        num_scalar_prefetch=1,
        grid=(n_seq,),
        in_specs=[row_spec, row_spec, row_spec, any_spec, any_spec],
        out_specs=row_spec,
        scratch_shapes=[pltpu.VMEM((2, n_pages, PAGE_SIZE * N_HEADS, HEAD_DIM), F32),
                        pltpu.VMEM((2, n_pages, PAGE_SIZE * N_HEADS, HEAD_DIM), F32),
                        pltpu.SemaphoreType.DMA((2, 2)),
                        pltpu.VMEM((n_pages * PAGE_SIZE, LANES), F32),
                        pltpu.VMEM((2 * SUBLANES, D_MODEL), F32)],
    )
    return pl.pallas_call(
        functools.partial(_sattn_body, n_pages, dec_seq),
        grid_spec=grid_spec,
        out_shape=jax.ShapeDtypeStruct((n_seq, dec_seq, D_MODEL), F32),
        compiler_params=_params(),
        name="moba_sample_attn",
    )(page_table.reshape(-1), q_s, k_s, v_s, cache_k, cache_v)


def _rope_tables(seq, past_len, dec_batch, dec_seq):
    half = ROT_DIM // 2
    pos = np.concatenate([np.arange(seq), past_len + np.repeat(np.arange(dec_seq), dec_batch)]).astype(np.float64)
    inv_freq = ROPE_THETA ** (-np.arange(half, dtype=np.float64) / half)
    ang = pos[:, None] * inv_freq[None, :]
    cos, sin = np.cos(ang), np.sin(ang)
    ones = np.ones((pos.shape[0], LANES - ROT_DIM))
    cos_t = np.concatenate([cos, cos, ones], axis=1).astype(np.float32)
    sin_t = np.concatenate([-sin, sin, 0.0 * ones], axis=1).astype(np.float32)
    return jnp.asarray(cos_t), jnp.asarray(sin_t)


def _to_time_major(a):
    return jnp.swapaxes(a, 0, 1).reshape((a.shape[0] * a.shape[1],) + a.shape[2:])


def _from_time_major(a, dec_batch, dec_seq):
    return jnp.swapaxes(a.reshape((dec_seq, dec_batch) + a.shape[1:]), 0, 1)


def kernel(x_prompt, x_sample, cache_k, cache_v, state_pool, page_table, p_prompt, p_sample, norm_ffn1, norm_mix, norm_ffn2, norm_ple, norm_final, ffn1_w_gate, ffn1_w_up, ffn1_w_down, ffn2_w_gate, ffn2_w_up, ffn2_w_down, ple_w_gate, ple_w_proj, a_w_in, a_ln_g, a_ln_b, a_w_s, a_b_s, a_w_out, b_w_qkv, b_w_out, c_w_in, c_w_grp, c_scale, c_w_out):
    batch, seq, d = x_prompt.shape
    dec_batch, dec_seq, _ = x_sample.shape
    n_prompt = batch * seq
    n_sample = dec_batch * dec_seq
    n_pages = page_table.shape[1]
    past_len = n_pages * PAGE_SIZE
    assert d == D_MODEL and n_sample == ROW_TILE and n_prompt % ROW_TILE == 0
    assert past_len % MOBA_BLOCK == 0

    x = (x_prompt.reshape(n_prompt, d), _to_time_major(x_sample))
    p_rows_prompt = p_prompt.reshape(DEPTH, n_prompt, PLE_DIM)
    p_rows_sample = jnp.swapaxes(p_sample, 1, 2).reshape(DEPTH, n_sample, PLE_DIM)
    rows = lambda v: v.reshape(v.shape[0], 1, v.shape[1])
    bf = lambda w: w.astype(BF16)
    g_ffn1, g_mix, g_ffn2, g_ple = rows(norm_ffn1), rows(norm_mix), rows(norm_ffn2), rows(norm_ple)
    w1 = (ffn1_w_gate, ffn1_w_up, ffn1_w_down)
    w2 = (ffn2_w_gate, ffn2_w_up, ffn2_w_down)
    weights = tuple(bf(w[0]) for w in w1)
    ple = (g_ple, bf(ple_w_gate), p_rows_prompt, p_rows_sample, bf(ple_w_proj))
    a_w = (bf(a_w_in), rows(a_ln_g), rows(a_ln_b), a_w_s)
    c_w = (bf(c_w_in), bf(c_w_grp), rows(c_scale), bf(c_w_out))
    w_qkv, w_attn_out, a_out = bf(b_w_qkv), bf(b_w_out), bf(a_w_out)

    k_p, v_p, k_s, v_s, pool_p, pool_s, chunk_v = [], [], [], [], [], [], []
    for i in range(DEPTH):
        kind, j = LAYER_MIXERS[i], LAYER_SLOT[i]
        x, weights = _ffn(x, i, g_ffn1, weights, n_prompt, cast_next=(*w2, i))
        attn = None
        if kind == 0:
            bexp = jnp.repeat(a_b_s[j].T, LANES, axis=1)
            wrow = jnp.repeat(jnp.transpose(a_w_s[j][:, :dec_seq, :dec_seq], (1, 2, 0)), LANES, axis=2)
            brow = jnp.repeat(a_b_s[j][:, :dec_seq].T, LANES, axis=1)
            x, vs = _mixer_a(x, i, j, g_mix, *a_w, bexp, wrow, brow, a_out, n_prompt, dec_batch, dec_seq)
            chunk_v.append(_from_time_major(vs, dec_batch, dec_seq))
        elif kind == 1:
            cos_t, sin_t = _rope_tables(seq, past_len, dec_batch, dec_seq)
            q_all, kp, vp, ks, vs = _qkv(x, i, j, g_mix, w_qkv, cos_t, sin_t, n_prompt, seq)
            o_p = _prompt_attention(q_all, kp, vp, batch, seq)
            q_s = _from_time_major(q_all[n_prompt:], dec_batch, dec_seq)
            ks_n = _from_time_major(ks, dec_batch, dec_seq)
            vs_n = _from_time_major(vs, dec_batch, dec_seq)
            o_s = _sample_attention(page_table, q_s, ks_n, vs_n,
                                    cache_k[j].reshape(-1, PAGE_SIZE * N_HEADS, HEAD_DIM),
                                    cache_v[j].reshape(-1, PAGE_SIZE * N_HEADS, HEAD_DIM))
            attn = (o_p, _to_time_major(o_s).astype(BF16), w_attn_out, j)
            k_p.append(kp.reshape(batch, seq, N_HEADS, HEAD_DIM))
            v_p.append(vp.reshape(batch, seq, N_HEADS, HEAD_DIM))
            k_s.append(ks_n.reshape(dec_batch, dec_seq, N_HEADS, HEAD_DIM))
            v_s.append(vs_n.reshape(dec_batch, dec_seq, N_HEADS, HEAD_DIM))
        else:
            hist_t = jnp.swapaxes(state_pool[j], 0, 1)
            x, tails, us = _mixer_c(x, i, j, g_mix, *c_w, hist_t, n_prompt, seq, dec_batch, dec_seq)
            tps = seq // ROW_TILE
            pool_p.append(tails[tps - 1:batch * tps:tps, HIST_PAD - POOL_HIST:, :])
            u_new = _from_time_major(us, dec_batch, dec_seq)
            pool_s.append(jnp.concatenate([state_pool[j], u_new], axis=1)[:, -POOL_HIST:])
        last = i + 1 == DEPTH
        x, weights = _ffn(x, i, g_ffn2, weights, n_prompt, attn=attn, ple=ple,
                          final_gain=norm_final.reshape(1, -1) if last else None,
                          cast_next=None if last else (*w1, i + 1))
    y_prompt, y_sample = x
    y_prompt = y_prompt.reshape(batch, seq, d)
    y_sample = _from_time_major(y_sample, dec_batch, dec_seq)
    return (y_prompt, y_sample, jnp.stack(k_p), jnp.stack(v_p), jnp.stack(k_s), jnp.stack(v_s),
            jnp.stack(pool_p), jnp.stack(pool_s), jnp.stack(chunk_v))
```

```python
import functools
import math

import jax
import jax.numpy as jnp
import numpy as np
from jax import lax
from jax.experimental import pallas as pl
from jax.experimental.pallas import tpu as pltpu

F32 = jnp.float32
BF16 = jnp.bfloat16

D_MODEL = 1024
DEPTH = 4
D_FF = 2816
PLE_DIM = 256
RMS_EPS = 1e-6
LN_EPS = 1e-5
CHUNK = 128
A_GROUPS = 8
N_HEADS = 8
HEAD_DIM = 128
ROT_DIM = 32
ROPE_THETA = 500000.0
MOBA_BLOCK = 256
MOBA_TOP_K = 3
PAGE_SIZE = 128
POOL_WINDOWS = (2, 4, 8, 16)
C_GROUP_DIM = 256
POOL_HIST = 15
LAYER_MIXERS = (0, 1, 2, 0)
LAYER_SLOT = (0, 0, 0, 1)

LANES = 128
SUBLANES = 8
VMEM_LIMIT = 56 * 1024 * 1024

ROW_TILE = 512
MIX_ROW_GROUPS = 2
FFN_ROW_GROUPS = 2
NEG_BIG = -1e30


def _dot(a, b):
    return jnp.dot(a, b, preferred_element_type=F32)


def _dot_t0(a, b):
    return lax.dot_general(a, b, (((0,), (0,)), ((), ())), preferred_element_type=F32)


def _dot_t1(a, b):
    return lax.dot_general(a, b, (((1,), (1,)), ((), ())), preferred_element_type=F32)


def _rms(x, g):
    return x * lax.rsqrt(jnp.mean(x * x, axis=-1, keepdims=True) + RMS_EPS) * g


def _split_bf16(x):
    hi = x.astype(BF16)
    lo = (x - hi.astype(F32)).astype(BF16)
    return hi, lo


def _const_spec(shape):
    nd = len(shape)
    return pl.BlockSpec(shape, lambda *_: (0,) * nd, pipeline_mode=pl.Buffered(1))


def _layer_spec(shape, layer):
    nd = len(shape)
    return pl.BlockSpec((None,) + tuple(shape), lambda *_: (layer,) + (0,) * nd, pipeline_mode=pl.Buffered(1))


def _row_spec(width):
    return pl.BlockSpec((ROW_TILE, width), lambda i: (i, 0))


def _params(n_axes=1):
    return pltpu.CompilerParams(dimension_semantics=("arbitrary",) * n_axes,
                                vmem_limit_bytes=VMEM_LIMIT)


def _ffn_body(n_prompt_tiles, split_x, attn, ple, final, n_cast, *refs):
    refs = list(refs)
    is_prompt = pl.program_id(0) < n_prompt_tiles
    take = lambda n: [refs.pop(0) for _ in range(n)]
    x_refs = take(2 if split_x else 1)
    attn_refs = take(3 if attn else 0)
    g_ref, wg_ref, wu_ref, wd_ref = take(4)
    ple_refs = take(5 if ple else 0)
    final_refs = take(1 if final else 0)
    cast_in = take(n_cast)
    out_refs = take(2 if final else 1)
    cast_out = refs

    for src, dst in zip(cast_in, cast_out):
        dst[...] = src[...].astype(BF16)

    def pick(prompt_ref, sample_ref, rows):
        return jnp.where(is_prompt, prompt_ref[rows, :], sample_ref[rows, :])

    def expand(rows):
        x = pick(*x_refs, rows) if split_x else x_refs[0][rows, :]
        if attn:
            op_ref, os_ref, wo_ref = attn_refs
            x = x + _dot(pick(op_ref, os_ref, rows), wo_ref[...])
        h = _rms(x, g_ref[...]).astype(BF16)
        return x, _dot(h, wg_ref[...]), _dot(h, wu_ref[...])

    def contract(x, gate, up):
        a = (gate * jax.nn.sigmoid(gate) * up).astype(BF16)
        return x + 0.5 * _dot(a, wd_ref[...])

    def embed(rows, x):
        if ple:
            gp_ref, wgate_ref, pp_ref, ps_ref, wproj_ref = ple_refs
            h = _rms(x, gp_ref[...]).astype(BF16)
            x = x + jax.nn.sigmoid(_dot(h, wgate_ref[...])) * _dot(pick(pp_ref, ps_ref, rows).astype(BF16),
                                                                    wproj_ref[...])
        return _rms(x, final_refs[0][...]) if final else x

    group_rows = ROW_TILE // FFN_ROW_GROUPS
    groups = [slice(r * group_rows, (r + 1) * group_rows) for r in range(FFN_ROW_GROUPS)]
    staged = [expand(rows) for rows in groups]
    staged = [contract(*s) for s in staged]
    staged = [embed(rows, x) for rows, x in zip(groups, staged)]
    if not final:
        for rows, x in zip(groups, staged):
            out_refs[0][rows, :] = x
        return
    yp_ref, ys_ref = out_refs

    @pl.when(is_prompt)
    def _():
        for rows, y in zip(groups, staged):
            yp_ref[rows, :] = y

    @pl.when(jnp.logical_not(is_prompt))
    def _():
        for rows, y in zip(groups, staged):
            ys_ref[rows, :] = y


WEIGHT_CAST_CHUNKS = 16


def _ffn(x, layer, gains, weights, n_prompt, attn=None, ple=None, final_gain=None, cast_next=None):
    wg, wu, wd = weights
    npt = n_prompt // ROW_TILE
    prompt_rows = lambda i: (jnp.minimum(i, npt - 1), 0)
    sample_rows = lambda i: (0, 0)
    split_x = isinstance(x, tuple)
    if split_x:
        args = list(x)
        in_specs = [pl.BlockSpec((ROW_TILE, D_MODEL), prompt_rows), pl.BlockSpec((ROW_TILE, D_MODEL), sample_rows)]
        n = n_prompt + x[1].shape[0]
    else:
        args = [x]
        in_specs = [_row_spec(D_MODEL)]
        n = x.shape[0]
    assert n - n_prompt == ROW_TILE
    if attn is not None:
        o_p, o_s, w_out, slot = attn
        args += [o_p, o_s, w_out]
        in_specs += [pl.BlockSpec((ROW_TILE, D_MODEL), prompt_rows), pl.BlockSpec((ROW_TILE, D_MODEL), sample_rows),
                     _layer_spec((D_MODEL, D_MODEL), slot)]
    args += [gains, wg, wu, wd]
    in_specs += [_layer_spec((1, D_MODEL), layer), _const_spec((D_MODEL, D_FF)),
                 _const_spec((D_MODEL, D_FF)), _const_spec((D_FF, D_MODEL))]
    if ple is not None:
        gp, wgate, p_prompt, p_sample, wproj = ple
        args += [gp, wgate, p_prompt, p_sample, wproj]
        in_specs += [_layer_spec((1, D_MODEL), layer), _layer_spec((D_MODEL, D_MODEL), layer),
                     pl.BlockSpec((None, ROW_TILE, PLE_DIM), lambda i: (layer, jnp.minimum(i, npt - 1), 0)),
                     pl.BlockSpec((None, ROW_TILE, PLE_DIM), lambda i: (layer, 0, 0)),
                     _layer_spec((PLE_DIM, D_MODEL), layer)]
    if final_gain is None:
        out_specs = [_row_spec(D_MODEL)]
        out_shape = [jax.ShapeDtypeStruct((n, D_MODEL), F32)]
    else:
        args.append(final_gain)
        in_specs.append(_const_spec((1, D_MODEL)))
        out_specs = [pl.BlockSpec((ROW_TILE, D_MODEL), prompt_rows), pl.BlockSpec((ROW_TILE, D_MODEL), sample_rows)]
        out_shape = [jax.ShapeDtypeStruct((n_prompt, D_MODEL), F32), jax.ShapeDtypeStruct((ROW_TILE, D_MODEL), F32)]
    n_main = len(out_shape)
    if cast_next:
        assert n // ROW_TILE >= WEIGHT_CAST_CHUNKS
        for w, layer_sel in cast_next:
            n_out = w.shape[1] if layer_sel is not None else w.shape[0] * w.shape[1]
            first = (layer_sel or 0) * WEIGHT_CAST_CHUNKS
            rows, cols = n_out // WEIGHT_CAST_CHUNKS, w.shape[2]
            assert rows * WEIGHT_CAST_CHUNKS == n_out and rows % (2 * SUBLANES) == 0
            chunk = lambda i: jnp.minimum(i, WEIGHT_CAST_CHUNKS - 1)
            args.append(w.reshape(-1, cols))
            in_specs.append(pl.BlockSpec((rows, cols), lambda i, chunk=chunk, first=first: (first + chunk(i), 0)))
            out_specs.append(pl.BlockSpec((rows, cols), lambda i, chunk=chunk: (chunk(i), 0)))
            out_shape.append(jax.ShapeDtypeStruct((n_out, cols), BF16))
    outs = pl.pallas_call(
        functools.partial(_ffn_body, npt, split_x, attn is not None, ple is not None, final_gain is not None,
                          len(cast_next) if cast_next else 0),
        grid=(n // ROW_TILE,),
        in_specs=in_specs,
        out_specs=out_specs,
        out_shape=out_shape,
        compiler_params=_params(),
        name="ffn_ple" if ple is not None else "ffn",
    )(*args)
    main = outs[0] if n_main == 1 else tuple(outs[:n_main])
    return main, tuple(outs[n_main:])


def _mixa_body(n_prompt_tiles, dec_batch, dec_seq,
               x_ref, g_ref, win_ref, lng_ref, lnb_ref, ws_ref, bexp_ref, wrow_ref, brow_ref, wout_ref,
               o_ref, vs_ref, v_scr, m_scr):
    i = pl.program_id(0)

    def project(rows):
        return _dot(_rms(x_ref[rows, :], g_ref[...]).astype(BF16), win_ref[...])

    def activate(rows, z):
        z = 0.5 * z * (1.0 + lax.erf(z * (1.0 / math.sqrt(2.0))))
        v = z[:, D_MODEL:]
        mu = jnp.mean(v, axis=-1, keepdims=True)
        vc = v - mu
        var = jnp.mean(vc * vc, axis=-1, keepdims=True)
        v_scr[rows, :] = vc * lax.rsqrt(var + LN_EPS) * lng_ref[...] + lnb_ref[...]
        return z[:, :D_MODEL]

    def back(rows, u):
        o_ref[rows, :] = x_ref[rows, :] + _dot((u * m_scr[rows, :]).astype(BF16), wout_ref[...])

    @pl.when(i < n_prompt_tiles)
    def _():
        r_iota = lax.broadcasted_iota(jnp.int32, (CHUNK, CHUNK), 0)
        c_iota = lax.broadcasted_iota(jnp.int32, (CHUNK, CHUNK), 1)
        w_tril = [jnp.where(c_iota <= r_iota, ws_ref[g], 0.0).astype(BF16) for g in range(A_GROUPS)]
        group_rows = ROW_TILE // MIX_ROW_GROUPS
        groups = [slice(r * group_rows, (r + 1) * group_rows) for r in range(MIX_ROW_GROUPS)]

        def mix(r):
            chunks = range(r * group_rows // CHUNK, (r + 1) * group_rows // CHUNK)
            for g in range(A_GROUPS):
                gs = slice(g * LANES, (g + 1) * LANES)
                rhs = jnp.concatenate([v_scr[c * CHUNK:(c + 1) * CHUNK, gs] for c in chunks], axis=1).astype(BF16)
                mg = _dot(w_tril[g], rhs)
                for k, c in enumerate(chunks):
                    m_scr[c * CHUNK:(c + 1) * CHUNK, gs] = mg[:, k * LANES:(k + 1) * LANES] + bexp_ref[:, gs]

        z = project(groups[0])
        for r in range(MIX_ROW_GROUPS):
            z_next = project(groups[r + 1]) if r + 1 < MIX_ROW_GROUPS else None
            u = activate(groups[r], z)
            mix(r)
            back(groups[r], u)
            z = z_next

    @pl.when(i >= n_prompt_tiles)
    def _():
        rows = slice(0, ROW_TILE)
        u = activate(rows, project(rows))
        for t in range(dec_seq):
            acc = brow_ref[t:t + 1, :] + wrow_ref[t, 0:1, :] * v_scr[0:dec_batch, :]
            for s in range(1, t + 1):
                acc = acc + wrow_ref[t, s:s + 1, :] * v_scr[s * dec_batch:(s + 1) * dec_batch, :]
            m_scr[t * dec_batch:(t + 1) * dec_batch, :] = acc
        vs_ref[...] = v_scr[...]
        back(rows, u)


def _mixer_a(x, layer, slot, g, w_in, ln_g, ln_b, w_s, bexp, wrow, brow, w_out, n_prompt, dec_batch, dec_seq):
    n = x.shape[0]
    assert n - n_prompt == ROW_TILE == dec_batch * dec_seq and dec_batch % SUBLANES == 0
    body = functools.partial(_mixa_body, n_prompt // ROW_TILE, dec_batch, dec_seq)
    return pl.pallas_call(
        body,
        grid=(n // ROW_TILE,),
        in_specs=[_row_spec(D_MODEL), _layer_spec((1, D_MODEL), layer), _layer_spec((D_MODEL, 2 * D_MODEL), slot),
                  _layer_spec((1, D_MODEL), slot), _layer_spec((1, D_MODEL), slot),
                  _layer_spec((A_GROUPS, CHUNK, CHUNK), slot), _const_spec((CHUNK, D_MODEL)),
                  _const_spec((dec_seq, dec_seq, D_MODEL)), _const_spec((dec_seq, D_MODEL)),
                  _layer_spec((D_MODEL, D_MODEL), slot)],
        out_specs=[_row_spec(D_MODEL), pl.BlockSpec((ROW_TILE, D_MODEL), lambda i: (0, 0))],
        out_shape=[jax.ShapeDtypeStruct((n, D_MODEL), F32),
                   jax.ShapeDtypeStruct((ROW_TILE, D_MODEL), F32)],
        scratch_shapes=[pltpu.VMEM((ROW_TILE, D_MODEL), F32), pltpu.VMEM((ROW_TILE, D_MODEL), F32)],
        compiler_params=_params(),
        name="mixer_a",
    )(x, g, w_in, ln_g, ln_b, w_s, bexp, wrow, brow, w_out)


HIST_PAD = 16


def _mixc_body(n_prompt_tiles, tiles_per_seq, dec_batch, dec_seq,
               x_ref, g_ref, win_ref, wgrp_ref, scale_ref, wout_ref, hist_ref,
               o_ref, tail_ref, us_ref, ext_scr, pool_scr):
    i = pl.program_id(0)

    def project(rows):
        return _dot(_rms(x_ref[rows, :], g_ref[...]).astype(BF16), win_ref[...])

    def finish(rows):
        parts = []
        for g in range(len(POOL_WINDOWS)):
            gs = slice(g * C_GROUP_DIM, (g + 1) * C_GROUP_DIM)
            parts.append(_dot(pool_scr[rows, gs].astype(BF16), wgrp_ref[g]))
        mixed = jnp.concatenate(parts, axis=1) * scale_ref[...]
        o_ref[rows, :] = x_ref[rows, :] + _dot(mixed.astype(BF16), wout_ref[...])

    @pl.when(i < n_prompt_tiles)
    def _():
        @pl.when(i % tiles_per_seq == 0)
        def _():
            ext_scr[0:HIST_PAD, :] = jnp.zeros((HIST_PAD, D_MODEL), F32)

        group_rows = ROW_TILE // MIX_ROW_GROUPS

        def pool(r):
            lo = r * group_rows
            pos = (i % tiles_per_seq) * ROW_TILE + lo + lax.broadcasted_iota(jnp.int32, (group_rows, 1), 0)
            for g, win in enumerate(POOL_WINDOWS):
                gs = slice(g * C_GROUP_DIM, (g + 1) * C_GROUP_DIM)
                cur = ext_scr[HIST_PAD + lo:HIST_PAD + lo + group_rows, gs]
                tot = cur
                for d in range(1, win):
                    tot = tot + ext_scr[HIST_PAD + lo - d:HIST_PAD + lo - d + group_rows, gs]
                inv = 1.0 / jnp.minimum(pos + 1, win).astype(F32)
                pool_scr[lo:lo + group_rows, gs] = tot * inv - cur

        for r in range(MIX_ROW_GROUPS):
            if r == 0:
                ext_scr[HIST_PAD:HIST_PAD + group_rows, :] = project(slice(0, group_rows))
            if r + 1 < MIX_ROW_GROUPS:
                nxt = slice((r + 1) * group_rows, (r + 2) * group_rows)
                ext_scr[HIST_PAD + nxt.start:HIST_PAD + nxt.stop, :] = project(nxt)
            pool(r)
            finish(slice(r * group_rows, (r + 1) * group_rows))
        tail = ext_scr[ROW_TILE:, :]
        tail_ref[0] = tail
        ext_scr[0:HIST_PAD, :] = tail

    @pl.when(i >= n_prompt_tiles)
    def _():
        u = project(slice(0, ROW_TILE))
        ext_scr[0:ROW_TILE, :] = u
        us_ref[...] = u
        tail_ref[0] = u[ROW_TILE - HIST_PAD:, :]

        def ext_row(j, gs):
            if j < POOL_HIST:
                return hist_ref[j, :, gs]
            return ext_scr[(j - POOL_HIST) * dec_batch:(j - POOL_HIST + 1) * dec_batch, gs]

        for t in range(dec_seq):
            for g, win in enumerate(POOL_WINDOWS):
                gs = slice(g * C_GROUP_DIM, (g + 1) * C_GROUP_DIM)
                cur = ext_row(POOL_HIST + t, gs)
                tot = cur
                for d in range(1, win):
                    tot = tot + ext_row(POOL_HIST + t - d, gs)
                pool_scr[t * dec_batch:(t + 1) * dec_batch, gs] = tot * (1.0 / win) - cur
        finish(slice(0, ROW_TILE))


def _mixer_c(x, layer, slot, g, w_in, w_grp, scale, w_out, hist_t, n_prompt, seq, dec_batch, dec_seq):
    n = x.shape[0]
    nt = n // ROW_TILE
    assert n - n_prompt == ROW_TILE == dec_batch * dec_seq and seq % ROW_TILE == 0
    assert POOL_HIST <= HIST_PAD and max(POOL_WINDOWS) - 1 <= POOL_HIST
    body = functools.partial(_mixc_body, n_prompt // ROW_TILE, seq // ROW_TILE, dec_batch, dec_seq)
    return pl.pallas_call(
        body,
        grid=(nt,),
        in_specs=[_row_spec(D_MODEL), _layer_spec((1, D_MODEL), layer), _layer_spec((D_MODEL, D_MODEL), slot),
                  _layer_spec((len(POOL_WINDOWS), C_GROUP_DIM, C_GROUP_DIM), slot), _layer_spec((1, D_MODEL), slot),
                  _layer_spec((D_MODEL, D_MODEL), slot), _const_spec((POOL_HIST, dec_batch, D_MODEL))],
        out_specs=[_row_spec(D_MODEL),
                   pl.BlockSpec((1, HIST_PAD, D_MODEL), lambda i: (i, 0, 0)),
                   pl.BlockSpec((ROW_TILE, D_MODEL), lambda i: (0, 0))],
        out_shape=[jax.ShapeDtypeStruct((n, D_MODEL), F32),
                   jax.ShapeDtypeStruct((nt, HIST_PAD, D_MODEL), F32),
                   jax.ShapeDtypeStruct((ROW_TILE, D_MODEL), F32)],
        scratch_shapes=[pltpu.VMEM((HIST_PAD + ROW_TILE, D_MODEL), F32),
                        pltpu.VMEM((ROW_TILE, D_MODEL), F32)],
        compiler_params=_params(),
        name="mixer_c",
    )(x, g, w_in, w_grp, scale, w_out, hist_t)


def _rope(xh, cos_t, sin_t, lane):
    partner = jnp.where(lane < ROT_DIM // 2, pltpu.roll(xh, LANES - ROT_DIM // 2, 1),
                        pltpu.roll(xh, ROT_DIM // 2, 1))
    return xh * cos_t + partner * sin_t


def _qkv_body(n_prompt_tiles, x_ref, g_ref, w_ref, cos_ref, sin_ref,
              q_ref, kp_ref, vp_ref, ks_ref, vs_ref):
    i = pl.program_id(0)
    group_rows = ROW_TILE // MIX_ROW_GROUPS
    groups = [slice(r * group_rows, (r + 1) * group_rows) for r in range(MIX_ROW_GROUPS)]
    lane = lax.broadcasted_iota(jnp.int32, (group_rows, LANES), 1)

    def project(rows):
        return _dot(_rms(x_ref[rows, :], g_ref[...]).astype(BF16), w_ref[...])

    def emit(k_ref, v_ref):
        def rotate(rows, qkv):
            cos_t = cos_ref[rows, :]
            sin_t = sin_ref[rows, :]
            for hd in range(N_HEADS):
                cols = slice(hd * HEAD_DIM, (hd + 1) * HEAD_DIM)
                q_ref[rows, cols] = _rope(qkv[:, cols], cos_t, sin_t, lane)
                off = D_MODEL + hd * HEAD_DIM
                k_ref[rows, cols] = _rope(qkv[:, off:off + HEAD_DIM], cos_t, sin_t, lane)
            v_ref[rows, :] = qkv[:, 2 * D_MODEL:]

        qkv = project(groups[0])
        for r in range(MIX_ROW_GROUPS):
            qkv_next = project(groups[r + 1]) if r + 1 < MIX_ROW_GROUPS else None
            rotate(groups[r], qkv)
            qkv = qkv_next

    pl.when(i < n_prompt_tiles)(functools.partial(emit, kp_ref, vp_ref))
    pl.when(i >= n_prompt_tiles)(functools.partial(emit, ks_ref, vs_ref))


def _qkv(x, layer, slot, g, w_qkv, cos_t, sin_t, n_prompt, seq):
    n = x.shape[0]
    npt = n_prompt // ROW_TILE
    tps = seq // ROW_TILE
    assert n - n_prompt == ROW_TILE and cos_t.shape[0] == seq + ROW_TILE
    prompt_spec = pl.BlockSpec((ROW_TILE, D_MODEL), lambda i: (jnp.minimum(i, npt - 1), 0))
    sample_spec = pl.BlockSpec((ROW_TILE, D_MODEL), lambda i: (0, 0))
    rope_spec = pl.BlockSpec((ROW_TILE, LANES), lambda i: (jnp.where(i < npt, i % tps, tps), 0))
    return pl.pallas_call(
        functools.partial(_qkv_body, npt),
        grid=(n // ROW_TILE,),
        in_specs=[_row_spec(D_MODEL), _layer_spec((1, D_MODEL), layer), _layer_spec((D_MODEL, 3 * D_MODEL), slot),
                  rope_spec, rope_spec],
        out_specs=[_row_spec(D_MODEL), prompt_spec, prompt_spec, sample_spec, sample_spec],
        out_shape=[jax.ShapeDtypeStruct((n, D_MODEL), F32),
                   jax.ShapeDtypeStruct((n_prompt, D_MODEL), F32),
                   jax.ShapeDtypeStruct((n_prompt, D_MODEL), F32),
                   jax.ShapeDtypeStruct((ROW_TILE, D_MODEL), F32),
                   jax.ShapeDtypeStruct((ROW_TILE, D_MODEL), F32)],
        compiler_params=_params(),
        name="moba_qkv",
    )(x, g, w_qkv, cos_t, sin_t)


def _top_k_mask(gate, valid, k, axis):
    idx = lax.broadcasted_iota(jnp.int32, gate.shape, axis)
    size = gate.shape[axis]
    g = jnp.where(valid, gate, -jnp.inf)
    chosen = jnp.zeros(gate.shape, dtype=jnp.bool_)
    for _ in range(k):
        mx = jnp.max(g, axis=axis, keepdims=True)
        first = jnp.min(jnp.where(g == mx, idx, size), axis=axis, keepdims=True)
        pick = (idx == first) & (mx > -jnp.inf)
        chosen = chosen | pick
        g = jnp.where(pick, -jnp.inf, g)
    return chosen


KV_CHUNK = 2
PREP_BLOCKS = 4
SUM_ROWS = 16


def _column_fold(x, op):
    return op(x.reshape(x.shape[0] // SUBLANES, SUBLANES, x.shape[1]), axis=0)


def _pattn_body(n_blocks, q_ref, k_ref, v_ref, o_ref, kaug_scr, vt_scr, qaug_scr, kmean_scr, s_scr):
    qp = pl.program_id(2)
    exp2_scale = (HEAD_DIM ** -0.5) * math.log2(math.e)
    chunk = KV_CHUNK * MOBA_BLOCK

    @pl.when(qp == 0)
    def _():
        lane = lax.broadcasted_iota(jnp.int32, (MOBA_BLOCK, LANES), 1)
        sum_rows = jnp.where(lax.broadcasted_iota(jnp.int32, (SUM_ROWS, MOBA_BLOCK), 0) == 0, 1.0, 0.0).astype(BF16)

        def fill(j, carry):
            rows = pl.ds(pl.multiple_of(j * MOBA_BLOCK, MOBA_BLOCK), MOBA_BLOCK)
            kj = k_ref[rows, :]
            kaug_scr[rows, 0:HEAD_DIM] = kj.astype(BF16)
            kaug_scr[rows, HEAD_DIM:] = jnp.where(lane == j, 1.0, 0.0).astype(BF16)
            vt_scr[j] = jnp.concatenate([v_ref[rows, :].T.astype(BF16), sum_rows], axis=0)
            kmean_scr[pl.ds(j, 1), :] = jnp.mean(kj, axis=0, keepdims=True)
            return carry

        lax.fori_loop(0, n_blocks, fill, 0, unroll=2)
        m_hi, m_lo = _split_bf16(kmean_scr[...])
        width = PREP_BLOCKS * MOBA_BLOCK
        blk = lax.broadcasted_iota(jnp.int32, (n_blocks, width), 0)
        col_blk = lax.broadcasted_iota(jnp.int32, (n_blocks, width), 1) // MOBA_BLOCK
        pad = jnp.zeros((LANES - n_blocks, MOBA_BLOCK), BF16)

        def prep(jj, carry):
            rows = pl.ds(pl.multiple_of(jj * width, width), width)
            qt = q_ref[rows, :].T
            qt_hi, qt_lo = _split_bf16(qt)
            gate = _dot(m_hi, qt_hi) + _dot(m_lo, qt_hi) + _dot(m_hi, qt_lo)
            allowed = _top_k_mask(gate, blk < col_blk + jj * PREP_BLOCKS, MOBA_TOP_K, 0)
            bias = jnp.where(allowed, 0.0, NEG_BIG).astype(BF16)
            qt_op = (qt * exp2_scale).astype(BF16)
            for i in range(PREP_BLOCKS):
                cols = slice(i * MOBA_BLOCK, (i + 1) * MOBA_BLOCK)
                qaug_scr[jj * PREP_BLOCKS + i] = jnp.concatenate([qt_op[:, cols], bias[:, cols], pad], axis=0)
            return carry

        lax.fori_loop(0, n_blocks // PREP_BLOCKS, prep, 0, unroll=2)

    blk_a = 2 * qp
    q_aug = jnp.concatenate([qaug_scr[blk_a], qaug_scr[blk_a + 1]], axis=1)
    own_a = pl.ds(pl.multiple_of(blk_a * MOBA_BLOCK, MOBA_BLOCK), MOBA_BLOCK)
    own_b = pl.ds(pl.multiple_of((blk_a + 1) * MOBA_BLOCK, MOBA_BLOCK), MOBA_BLOCK)

    def attend(k_chunks):
        def score(c):
            s_scr[c % 2] = _dot(kaug_scr[c * chunk:(c + 1) * chunk, :], q_aug)

        def values(c, pb):
            pv = _dot(vt_scr[c * KV_CHUNK], pb[0:MOBA_BLOCK, :])
            for j in range(1, KV_CHUNK):
                pv = pv + _dot(vt_scr[c * KV_CHUNK + j], pb[j * MOBA_BLOCK:(j + 1) * MOBA_BLOCK, :])
            return pv

        score(0)
        if k_chunks > 1:
            score(1)
        s_own = jnp.concatenate(
            [_dot(kaug_scr[own_a, 0:HEAD_DIM], q_aug[0:HEAD_DIM, 0:MOBA_BLOCK]),
             _dot(kaug_scr[own_b, 0:HEAD_DIM], q_aug[0:HEAD_DIM, MOBA_BLOCK:])], axis=1)
        key = lax.broadcasted_iota(jnp.int32, s_own.shape, 0)
        qry = lax.broadcasted_iota(jnp.int32, s_own.shape, 1) % MOBA_BLOCK
        s_own = jnp.where(key <= qry, s_own, NEG_BIG)
        s = s_scr[0]
        m8 = jnp.maximum(_column_fold(s_own, jnp.max), _column_fold(s, jnp.max))
        m = jnp.max(m8, axis=0, keepdims=True)
        pb_own = jnp.exp2(s_own - m).astype(BF16)
        acc = jnp.concatenate([_dot(vt_scr[blk_a], pb_own[:, 0:MOBA_BLOCK]),
                               _dot(vt_scr[blk_a + 1], pb_own[:, MOBA_BLOCK:])], axis=1)
        acc = acc + values(0, jnp.exp2(s - m).astype(BF16))
        for c in range(1, k_chunks):
            if c + 1 < k_chunks:
                score(c + 1)
            s = s_scr[c % 2]
            m_new = jnp.maximum(m, jnp.max(_column_fold(s, jnp.max), axis=0, keepdims=True))
            acc = jnp.exp2(m - m_new) * acc + values(c, jnp.exp2(s - m_new).astype(BF16))
            m = m_new
        o_ref[...] = (acc[0:HEAD_DIM, :] * (1.0 / acc[HEAD_DIM:HEAD_DIM + 1, :])).T.astype(o_ref.dtype)

    n_chunks = (blk_a + 1 + KV_CHUNK - 1) // KV_CHUNK
    for k_chunks in range(1, n_blocks // KV_CHUNK + 1):
        pl.when(n_chunks == k_chunks)(functools.partial(attend, k_chunks))


def _prompt_attention(q_all, k_p, v_p, batch, seq):
    n_blocks = seq // MOBA_BLOCK
    assert seq % MOBA_BLOCK == 0 and n_blocks <= LANES
    assert n_blocks % KV_CHUNK == 0 and n_blocks % PREP_BLOCKS == 0
    assert n_blocks % (2 * SUBLANES) == 0
    n_pairs = n_blocks // 2
    pairspec = pl.BlockSpec((2 * MOBA_BLOCK, HEAD_DIM), lambda b, h, i: (b * n_pairs + i, h))
    seqspec = pl.BlockSpec((seq, HEAD_DIM), lambda b, h, i: (b, h))
    return pl.pallas_call(
        functools.partial(_pattn_body, n_blocks),
        grid=(batch, N_HEADS, n_pairs),
        in_specs=[seqspec, seqspec, seqspec],
        out_specs=pairspec,
        out_shape=jax.ShapeDtypeStruct((batch * seq, D_MODEL), BF16),
        scratch_shapes=[pltpu.VMEM((seq, 2 * LANES), BF16),
                        pltpu.VMEM((n_blocks, HEAD_DIM + SUM_ROWS, MOBA_BLOCK), BF16),
                        pltpu.VMEM((n_blocks, 2 * LANES, MOBA_BLOCK), BF16),
                        pltpu.VMEM((n_blocks, HEAD_DIM), F32),
                        pltpu.VMEM((2, KV_CHUNK * MOBA_BLOCK, 2 * MOBA_BLOCK), F32)],
        compiler_params=_params(3),
        name="moba_prompt_attn",
    )(q_all, k_p, v_p)


def _sattn_body(n_pages, dec_seq, pt_ref, q_ref, kn_ref, vn_ref, ck_hbm, cv_hbm, o_ref,
                kbuf, vbuf, sem, s_scr, kmean_scr):
    n = pl.program_id(0)
    n_seq = pl.num_programs(0)
    slot = n % 2
    scale = HEAD_DIM ** -0.5
    pages_per_block = MOBA_BLOCK // PAGE_SIZE
    n_blocks = n_pages // pages_per_block

    def page_copy(hbm, buf, seq, sl, p, kind):
        return pltpu.make_async_copy(hbm.at[pt_ref[seq * n_pages + p]], buf.at[sl, p], sem.at[kind, sl])

    def start_all(seq, sl):
        for p in range(n_pages):
            page_copy(ck_hbm, kbuf, seq, sl, p, 0).start()
            page_copy(cv_hbm, vbuf, seq, sl, p, 1).start()

    @pl.when(n == 0)
    def _():
        start_all(0, 0)

    @pl.when(n + 1 < n_seq)
    def _():
        start_all(n + 1, 1 - slot)

    for p in range(n_pages):
        page_copy(ck_hbm, kbuf, n, slot, p, 0).wait()
        page_copy(cv_hbm, vbuf, n, slot, p, 1).wait()

    def page_rows(buf, p):
        return jnp.concatenate([buf[slot, p, pl.ds(hd, PAGE_SIZE, stride=N_HEADS), :]
                                for hd in range(N_HEADS)], axis=1)

    pad_rows = jnp.zeros((PAGE_SIZE - dec_seq, D_MODEL), F32)
    q_pad = jnp.concatenate([q_ref[0], pad_rows], axis=0)
    row_t = lax.broadcasted_iota(jnp.int32, (PAGE_SIZE, LANES), 0)
    col_c = lax.broadcasted_iota(jnp.int32, (PAGE_SIZE, LANES), 1)
    pick = jnp.where((col_c % dec_seq == row_t) & (col_c < N_HEADS * dec_seq), 1.0, 0.0).astype(BF16)
    head_r = lax.broadcasted_iota(jnp.int32, (D_MODEL, LANES), 0) // HEAD_DIM
    head_c = lax.broadcasted_iota(jnp.int32, (D_MODEL, LANES), 1) // dec_seq
    same_head = head_r == head_c
    q_hi, q_lo = _split_bf16(q_pad)
    qbd_hi = jnp.where(same_head, _dot_t0(q_hi, pick), 0.0).astype(BF16)
    qbd_lo = jnp.where(same_head, _dot_t0(q_lo, pick), 0.0).astype(BF16)

    kmean_scr[...] = jnp.zeros(kmean_scr.shape, F32)
    for b in range(n_blocks):
        ksum = None
        for r in range(pages_per_block):
            p = b * pages_per_block + r
            kp = page_rows(kbuf, p)
            s_scr[p * PAGE_SIZE:(p + 1) * PAGE_SIZE, :] = _dot(kp.astype(BF16), qbd_hi) * scale
            part = jnp.sum(kp, axis=0, keepdims=True)
            ksum = part if ksum is None else ksum + part
        kmean_scr[b:b + 1, :] = ksum * (1.0 / MOBA_BLOCK)
    m_hi, m_lo = _split_bf16(kmean_scr[...])
    gate = _dot(m_hi, qbd_hi) + _dot(m_lo, qbd_hi) + _dot(m_hi, qbd_lo)
    blk = lax.broadcasted_iota(jnp.int32, gate.shape, 0)
    allowed = _top_k_mask(gate, blk < n_blocks, min(MOBA_TOP_K, n_blocks), 0)
    bias = jnp.where(allowed, 0.0, NEG_BIG)

    k_pad = jnp.concatenate([kn_ref[0], pad_rows], axis=0)
    v_pad = jnp.concatenate([vn_ref[0], pad_rows], axis=0)
    s_new = _dot(k_pad.astype(BF16), qbd_hi) * scale
    s_new = jnp.where((row_t <= col_c % dec_seq) & (row_t < dec_seq), s_new, NEG_BIG)

    m = jnp.max(s_new, axis=0, keepdims=True)
    for b in range(n_blocks):
        rows = slice(b * MOBA_BLOCK, (b + 1) * MOBA_BLOCK)
        sb = s_scr[rows, :] + bias[b:b + 1, :]
        s_scr[rows, :] = sb
        m = jnp.maximum(m, jnp.max(sb, axis=0, keepdims=True))
    p_new = jnp.exp(s_new - m)
    l = jnp.sum(p_new, axis=0, keepdims=True)
    for p in range(n_pages):
        rows = slice(p * PAGE_SIZE, (p + 1) * PAGE_SIZE)
        pp = jnp.exp(s_scr[rows, :] - m)
        s_scr[rows, :] = pp
        l = l + jnp.sum(pp, axis=0, keepdims=True)
    inv = 1.0 / l
    used = slice(0, N_HEADS * dec_seq)
    out = _dot_t0((p_new * inv)[:, used].astype(BF16), v_pad.astype(BF16))
    for p in range(n_pages):
        rows = slice(p * PAGE_SIZE, (p + 1) * PAGE_SIZE)
        out = out + _dot_t0((s_scr[rows, :] * inv)[:, used].astype(BF16), page_rows(vbuf, p).astype(BF16))
    out_c = lax.broadcasted_iota(jnp.int32, out.shape, 0)
    out_h = lax.broadcasted_iota(jnp.int32, out.shape, 1) // HEAD_DIM
    for t in range(dec_seq):
        keep = (out_c % dec_seq == t) & (out_c // dec_seq == out_h)
        o_ref[0, t:t + 1, :] = jnp.sum(jnp.where(keep, out, 0.0), axis=0, keepdims=True)


def _sample_attention(page_table, q_s, k_s, v_s, cache_k, cache_v):
    n_seq, n_pages = page_table.shape
    dec_seq = q_s.shape[1]
    assert n_pages % (MOBA_BLOCK // PAGE_SIZE) == 0 and N_HEADS * dec_seq <= LANES and dec_seq <= SUBLANES
    row_spec = pl.BlockSpec((1, dec_seq, D_MODEL), lambda n, pt: (n, 0, 0))
    any_spec = pl.BlockSpec(memory_space=pl.ANY)
    grid_spec = pltpu.PrefetchScalarGridSpec(
        num_scalar_prefetch=1,
        grid=(n_seq,),
        in_specs=[row_spec, row_spec, row_spec, any_spec, any_spec],
        out_specs=row_spec,
        scratch_shapes=[pltpu.VMEM((2, n_pages, PAGE_SIZE * N_HEADS, HEAD_DIM), F32),
                        pltpu.VMEM((2, n_pages, PAGE_SIZE * N_HEADS, HEAD_DIM), F32),
                        pltpu.SemaphoreType.DMA((2, 2)),
                        pltpu.VMEM((n_pages * PAGE_SIZE, LANES), F32),
                        pltpu.VMEM((2 * SUBLANES, D_MODEL), F32)],
    )
    return pl.pallas_call(
        functools.partial(_sattn_body, n_pages, dec_seq),
        grid_spec=grid_spec,
        out_shape=jax.ShapeDtypeStruct((n_seq, dec_seq, D_MODEL), F32),
        compiler_params=_params(),
        name="moba_sample_attn",
    )(page_table.reshape(-1), q_s, k_s, v_s, cache_k, cache_v)


def _rope_tables(seq, past_len, dec_batch, dec_seq):
    half = ROT_DIM // 2
    pos = np.concatenate([np.arange(seq), past_len + np.repeat(np.arange(dec_seq), dec_batch)]).astype(np.float64)
    inv_freq = ROPE_THETA ** (-np.arange(half, dtype=np.float64) / half)
    ang = pos[:, None] * inv_freq[None, :]
    cos, sin = np.cos(ang), np.sin(ang)
    ones = np.ones((pos.shape[0], LANES - ROT_DIM))
    cos_t = np.concatenate([cos, cos, ones], axis=1).astype(np.float32)
    sin_t = np.concatenate([-sin, sin, 0.0 * ones], axis=1).astype(np.float32)
    return jnp.asarray(cos_t), jnp.asarray(sin_t)


def _to_time_major(a):
    return jnp.swapaxes(a, 0, 1).reshape((a.shape[0] * a.shape[1],) + a.shape[2:])


def _from_time_major(a, dec_batch, dec_seq):
    return jnp.swapaxes(a.reshape((dec_seq, dec_batch) + a.shape[1:]), 0, 1)


def kernel(x_prompt, x_sample, cache_k, cache_v, state_pool, page_table, p_prompt, p_sample, norm_ffn1, norm_mix, norm_ffn2, norm_ple, norm_final, ffn1_w_gate, ffn1_w_up, ffn1_w_down, ffn2_w_gate, ffn2_w_up, ffn2_w_down, ple_w_gate, ple_w_proj, a_w_in, a_ln_g, a_ln_b, a_w_s, a_b_s, a_w_out, b_w_qkv, b_w_out, c_w_in, c_w_grp, c_scale, c_w_out):
    batch, seq, d = x_prompt.shape
    dec_batch, dec_seq, _ = x_sample.shape
    n_prompt = batch * seq
    n_sample = dec_batch * dec_seq
    n_pages = page_table.shape[1]
    past_len = n_pages * PAGE_SIZE
    assert d == D_MODEL and n_sample == ROW_TILE and n_prompt % ROW_TILE == 0
    assert past_len % MOBA_BLOCK == 0

    x = (x_prompt.reshape(n_prompt, d), _to_time_major(x_sample))
    p_rows_prompt = p_prompt.reshape(DEPTH, n_prompt, PLE_DIM)
    p_rows_sample = jnp.swapaxes(p_sample, 1, 2).reshape(DEPTH, n_sample, PLE_DIM)
    rows = lambda v: v.reshape(v.shape[0], 1, v.shape[1])
    bf = lambda w: w.astype(BF16)
    g_ffn1, g_mix, g_ffn2, g_ple = rows(norm_ffn1), rows(norm_mix), rows(norm_ffn2), rows(norm_ple)
    w1 = (ffn1_w_gate, ffn1_w_up, ffn1_w_down)
    w2 = (ffn2_w_gate, ffn2_w_up, ffn2_w_down)
    weights = tuple(bf(w[0]) for w in w1)
    small = [ple_w_gate, ple_w_proj, a_w_in, a_w_out, b_w_qkv, b_w_out, c_w_in,
             c_w_grp.reshape(c_w_grp.shape[0], -1, C_GROUP_DIM), c_w_out]

    k_p, v_p, k_s, v_s, pool_p, pool_s, chunk_v = [], [], [], [], [], [], []
    for i in range(DEPTH):
        kind, j = LAYER_MIXERS[i], LAYER_SLOT[i]
        casts = [(w, i) for w in w2] + ([(w, None) for w in small] if i == 0 else [])
        x, narrowed = _ffn(x, i, g_ffn1, weights, n_prompt, cast_next=casts)
        weights = narrowed[:3]
        if i == 0:
            (ple_gate, ple_proj, a_in, a_out, w_qkv, w_attn_out, c_in, c_grp, c_out) = [
                nb.reshape(w.shape) for nb, w in zip(narrowed[3:], small)]
            ple = (g_ple, ple_gate, p_rows_prompt, p_rows_sample, ple_proj)
            a_w = (a_in, rows(a_ln_g), rows(a_ln_b), a_w_s)
            c_w = (c_in, c_grp.reshape(c_w_grp.shape), rows(c_scale), c_out)
        attn = None
        if kind == 0:
            bexp = jnp.repeat(a_b_s[j].T, LANES, axis=1)
            wrow = jnp.repeat(jnp.transpose(a_w_s[j][:, :dec_seq, :dec_seq], (1, 2, 0)), LANES, axis=2)
            brow = jnp.repeat(a_b_s[j][:, :dec_seq].T, LANES, axis=1)
            x, vs = _mixer_a(x, i, j, g_mix, *a_w, bexp, wrow, brow, a_out, n_prompt, dec_batch, dec_seq)
            chunk_v.append(_from_time_major(vs, dec_batch, dec_seq))
        elif kind == 1:
            cos_t, sin_t = _rope_tables(seq, past_len, dec_batch, dec_seq)
            q_all, kp, vp, ks, vs = _qkv(x, i, j, g_mix, w_qkv, cos_t, sin_t, n_prompt, seq)
            o_p = _prompt_attention(q_all, kp, vp, batch, seq)
            q_s = _from_time_major(q_all[n_prompt:], dec_batch, dec_seq)
            ks_n = _from_time_major(ks, dec_batch, dec_seq)
            vs_n = _from_time_major(vs, dec_batch, dec_seq)
            o_s = _sample_attention(page_table, q_s, ks_n, vs_n,
                                    cache_k[j].reshape(-1, PAGE_SIZE * N_HEADS, HEAD_DIM),
                                    cache_v[j].reshape(-1, PAGE_SIZE * N_HEADS, HEAD_DIM))
            attn = (o_p, _to_time_major(o_s).astype(BF16), w_attn_out, j)
            k_p.append(kp.reshape(batch, seq, N_HEADS, HEAD_DIM))
            v_p.append(vp.reshape(batch, seq, N_HEADS, HEAD_DIM))
            k_s.append(ks_n.reshape(dec_batch, dec_seq, N_HEADS, HEAD_DIM))
            v_s.append(vs_n.reshape(dec_batch, dec_seq, N_HEADS, HEAD_DIM))
        else:
            hist_t = jnp.swapaxes(state_pool[j], 0, 1)
            x, tails, us = _mixer_c(x, i, j, g_mix, *c_w, hist_t, n_prompt, seq, dec_batch, dec_seq)
            tps = seq // ROW_TILE
            pool_p.append(tails[tps - 1:batch * tps:tps, HIST_PAD - POOL_HIST:, :])
            u_new = _from_time_major(us, dec_batch, dec_seq)
            pool_s.append(jnp.concatenate([state_pool[j], u_new], axis=1)[:, -POOL_HIST:])
        last = i + 1 == DEPTH
        x, weights = _ffn(x, i, g_ffn2, weights, n_prompt, attn=attn, ple=ple,
                          final_gain=norm_final.reshape(1, -1) if last else None,
                          cast_next=None if last else [(w, i + 1) for w in w1])
    y_prompt, y_sample = x
    y_prompt = y_prompt.reshape(batch, seq, d)
    y_sample = _from_time_major(y_sample, dec_batch, dec_seq)
    return (y_prompt, y_sample, jnp.stack(k_p), jnp.stack(v_p), jnp.stack(k_s), jnp.stack(v_s),
            jnp.stack(pool_p), jnp.stack(pool_s), jnp.stack(chunk_v))
```

```python
import functools
import math

import jax
import jax.numpy as jnp
import numpy as np
from jax import lax
from jax.experimental import pallas as pl
from jax.experimental.pallas import tpu as pltpu

F32 = jnp.float32
BF16 = jnp.bfloat16

D_MODEL = 1024
DEPTH = 4
D_FF = 2816
PLE_DIM = 256
RMS_EPS = 1e-6
LN_EPS = 1e-5
CHUNK = 128
A_GROUPS = 8
N_HEADS = 8
HEAD_DIM = 128
ROT_DIM = 32
ROPE_THETA = 500000.0
MOBA_BLOCK = 256
MOBA_TOP_K = 3
PAGE_SIZE = 128
POOL_WINDOWS = (2, 4, 8, 16)
C_GROUP_DIM = 256
POOL_HIST = 15
LAYER_MIXERS = (0, 1, 2, 0)
LAYER_SLOT = (0, 0, 0, 1)

LANES = 128
SUBLANES = 8
VMEM_LIMIT = 56 * 1024 * 1024

ROW_TILE = 512
MIX_ROW_GROUPS = 2
FFN_ROW_GROUPS = 2
NEG_BIG = -1e30


def _dot(a, b):
    return jnp.dot(a, b, preferred_element_type=F32)


def _dot_t0(a, b):
    return lax.dot_general(a, b, (((0,), (0,)), ((), ())), preferred_element_type=F32)


def _dot_t1(a, b):
    return lax.dot_general(a, b, (((1,), (1,)), ((), ())), preferred_element_type=F32)


def _rms(x, g):
    return x * lax.rsqrt(jnp.mean(x * x, axis=-1, keepdims=True) + RMS_EPS) * g


def _split_bf16(x):
    hi = x.astype(BF16)
    lo = (x - hi.astype(F32)).astype(BF16)
    return hi, lo


def _const_spec(shape):
    nd = len(shape)
    return pl.BlockSpec(shape, lambda *_: (0,) * nd, pipeline_mode=pl.Buffered(1))


def _layer_spec(shape, layer):
    nd = len(shape)
    return pl.BlockSpec((None,) + tuple(shape), lambda *_: (layer,) + (0,) * nd, pipeline_mode=pl.Buffered(1))


def _row_spec(width):
    return pl.BlockSpec((ROW_TILE, width), lambda i: (i, 0))


def _params(n_axes=1):
    return pltpu.CompilerParams(dimension_semantics=("arbitrary",) * n_axes,
                                vmem_limit_bytes=VMEM_LIMIT)


def _ffn_body(n_prompt_tiles, split_x, attn, ple, final, n_cast, *refs):
    refs = list(refs)
    is_prompt = pl.program_id(0) < n_prompt_tiles
    take = lambda n: [refs.pop(0) for _ in range(n)]
    x_refs = take(2 if split_x else 1)
    attn_refs = take(3 if attn else 0)
    g_ref, wg_ref, wu_ref, wd_ref = take(4)
    ple_refs = take(5 if ple else 0)
    final_refs = take(1 if final else 0)
    cast_in = take(n_cast)
    out_refs = take(2 if final else 1)
    cast_out = refs

    for src, dst in zip(cast_in, cast_out):
        dst[...] = src[...].astype(BF16)

    def pick(prompt_ref, sample_ref, rows):
        return jnp.where(is_prompt, prompt_ref[rows, :], sample_ref[rows, :])

    def expand(rows):
        x = pick(*x_refs, rows) if split_x else x_refs[0][rows, :]
        if attn:
            op_ref, os_ref, wo_ref = attn_refs
            x = x + _dot(pick(op_ref, os_ref, rows), wo_ref[...])
        h = _rms(x, g_ref[...]).astype(BF16)
        return x, _dot(h, wg_ref[...]), _dot(h, wu_ref[...])

    def contract(x, gate, up):
        a = (gate * jax.nn.sigmoid(gate) * up).astype(BF16)
        return x + 0.5 * _dot(a, wd_ref[...])

    def embed(rows, x):
        if ple:
            gp_ref, wgate_ref, pp_ref, ps_ref, wproj_ref = ple_refs
            h = _rms(x, gp_ref[...]).astype(BF16)
            x = x + jax.nn.sigmoid(_dot(h, wgate_ref[...])) * _dot(pick(pp_ref, ps_ref, rows).astype(BF16),
                                                                    wproj_ref[...])
        return _rms(x, final_refs[0][...]) if final else x

    group_rows = ROW_TILE // FFN_ROW_GROUPS
    groups = [slice(r * group_rows, (r + 1) * group_rows) for r in range(FFN_ROW_GROUPS)]
    staged = [expand(rows) for rows in groups]
    staged = [contract(*s) for s in staged]
    staged = [embed(rows, x) for rows, x in zip(groups, staged)]
    if not final:
        for rows, x in zip(groups, staged):
            out_refs[0][rows, :] = x
        return
    yp_ref, ys_ref = out_refs

    @pl.when(is_prompt)
    def _():
        for rows, y in zip(groups, staged):
            yp_ref[rows, :] = y

    @pl.when(jnp.logical_not(is_prompt))
    def _():
        for rows, y in zip(groups, staged):
            ys_ref[rows, :] = y


WEIGHT_CAST_CHUNKS = 16


def _ffn(x, layer, gains, weights, n_prompt, attn=None, ple=None, final_gain=None, cast_next=None):
    wg, wu, wd = weights
    npt = n_prompt // ROW_TILE
    prompt_rows = lambda i: (jnp.minimum(i, npt - 1), 0)
    sample_rows = lambda i: (0, 0)
    split_x = isinstance(x, tuple)
    if split_x:
        args = list(x)
        in_specs = [pl.BlockSpec((ROW_TILE, D_MODEL), prompt_rows), pl.BlockSpec((ROW_TILE, D_MODEL), sample_rows)]
        n = n_prompt + x[1].shape[0]
    else:
        args = [x]
        in_specs = [_row_spec(D_MODEL)]
        n = x.shape[0]
    assert n - n_prompt == ROW_TILE
    if attn is not None:
        o_p, o_s, w_out, slot = attn
        args += [o_p, o_s, w_out]
        in_specs += [pl.BlockSpec((ROW_TILE, D_MODEL), prompt_rows), pl.BlockSpec((ROW_TILE, D_MODEL), sample_rows),
                     _layer_spec((D_MODEL, D_MODEL), slot)]
    args += [gains, wg, wu, wd]
    in_specs += [_layer_spec((1, D_MODEL), layer), _const_spec((D_MODEL, D_FF)),
                 _const_spec((D_MODEL, D_FF)), _const_spec((D_FF, D_MODEL))]
    if ple is not None:
        gp, wgate, p_prompt, p_sample, wproj = ple
        args += [gp, wgate, p_prompt, p_sample, wproj]
        in_specs += [_layer_spec((1, D_MODEL), layer), _layer_spec((D_MODEL, D_MODEL), layer),
                     pl.BlockSpec((None, ROW_TILE, PLE_DIM), lambda i: (layer, jnp.minimum(i, npt - 1), 0)),
                     pl.BlockSpec((None, ROW_TILE, PLE_DIM), lambda i: (layer, 0, 0)),
                     _layer_spec((PLE_DIM, D_MODEL), layer)]
    if final_gain is None:
        out_specs = [_row_spec(D_MODEL)]
        out_shape = [jax.ShapeDtypeStruct((n, D_MODEL), F32)]
    else:
        args.append(final_gain)
        in_specs.append(_const_spec((1, D_MODEL)))
        out_specs = [pl.BlockSpec((ROW_TILE, D_MODEL), prompt_rows), pl.BlockSpec((ROW_TILE, D_MODEL), sample_rows)]
        out_shape = [jax.ShapeDtypeStruct((n_prompt, D_MODEL), F32), jax.ShapeDtypeStruct((ROW_TILE, D_MODEL), F32)]
    n_main = len(out_shape)
    if cast_next:
        assert n // ROW_TILE >= WEIGHT_CAST_CHUNKS
        for w, layer_sel in cast_next:
            n_out = w.shape[1] if layer_sel is not None else w.shape[0] * w.shape[1]
            first = (layer_sel or 0) * WEIGHT_CAST_CHUNKS
            rows, cols = n_out // WEIGHT_CAST_CHUNKS, w.shape[2]
            assert rows * WEIGHT_CAST_CHUNKS == n_out and rows % (2 * SUBLANES) == 0
            chunk = lambda i: jnp.minimum(i, WEIGHT_CAST_CHUNKS - 1)
            args.append(w.reshape(-1, cols))
            in_specs.append(pl.BlockSpec((rows, cols), lambda i, chunk=chunk, first=first: (first + chunk(i), 0)))
            out_specs.append(pl.BlockSpec((rows, cols), lambda i, chunk=chunk: (chunk(i), 0)))
            out_shape.append(jax.ShapeDtypeStruct((n_out, cols), BF16))
    outs = pl.pallas_call(
        functools.partial(_ffn_body, npt, split_x, attn is not None, ple is not None, final_gain is not None,
                          len(cast_next) if cast_next else 0),
        grid=(n // ROW_TILE,),
        in_specs=in_specs,
        out_specs=out_specs,
        out_shape=out_shape,
        compiler_params=_params(),
        name="ffn_ple" if ple is not None else "ffn",
    )(*args)
    main = outs[0] if n_main == 1 else tuple(outs[:n_main])
    return main, tuple(outs[n_main:])


def _mixa_body(n_prompt_tiles, dec_batch, dec_seq,
               x_ref, g_ref, win_ref, lng_ref, lnb_ref, ws_ref, bexp_ref, wrow_ref, brow_ref, wout_ref,
               o_ref, vs_ref, v_scr, m_scr):
    i = pl.program_id(0)

    def project(rows):
        return _dot(_rms(x_ref[rows, :], g_ref[...]).astype(BF16), win_ref[...])

    def activate(rows, z):
        z = 0.5 * z * (1.0 + lax.erf(z * (1.0 / math.sqrt(2.0))))
        v = z[:, D_MODEL:]
        mu = jnp.mean(v, axis=-1, keepdims=True)
        vc = v - mu
        var = jnp.mean(vc * vc, axis=-1, keepdims=True)
        v_scr[rows, :] = vc * lax.rsqrt(var + LN_EPS) * lng_ref[...] + lnb_ref[...]
        return z[:, :D_MODEL]

    def back(rows, u):
        o_ref[rows, :] = x_ref[rows, :] + _dot((u * m_scr[rows, :]).astype(BF16), wout_ref[...])

    @pl.when(i < n_prompt_tiles)
    def _():
        r_iota = lax.broadcasted_iota(jnp.int32, (CHUNK, CHUNK), 0)
        c_iota = lax.broadcasted_iota(jnp.int32, (CHUNK, CHUNK), 1)
        w_tril = [jnp.where(c_iota <= r_iota, ws_ref[g], 0.0).astype(BF16) for g in range(A_GROUPS)]
        group_rows = ROW_TILE // MIX_ROW_GROUPS
        groups = [slice(r * group_rows, (r + 1) * group_rows) for r in range(MIX_ROW_GROUPS)]

        def mix(r):
            chunks = range(r * group_rows // CHUNK, (r + 1) * group_rows // CHUNK)
            for g in range(A_GROUPS):
                gs = slice(g * LANES, (g + 1) * LANES)
                rhs = jnp.concatenate([v_scr[c * CHUNK:(c + 1) * CHUNK, gs] for c in chunks], axis=1).astype(BF16)
                mg = _dot(w_tril[g], rhs)
                for k, c in enumerate(chunks):
                    m_scr[c * CHUNK:(c + 1) * CHUNK, gs] = mg[:, k * LANES:(k + 1) * LANES] + bexp_ref[:, gs]

        z = project(groups[0])
        for r in range(MIX_ROW_GROUPS):
            z_next = project(groups[r + 1]) if r + 1 < MIX_ROW_GROUPS else None
            u = activate(groups[r], z)
            mix(r)
            back(groups[r], u)
            z = z_next

    @pl.when(i >= n_prompt_tiles)
    def _():
        rows = slice(0, ROW_TILE)
        u = activate(rows, project(rows))
        for t in range(dec_seq):
            acc = brow_ref[t:t + 1, :] + wrow_ref[t, 0:1, :] * v_scr[0:dec_batch, :]
            for s in range(1, t + 1):
                acc = acc + wrow_ref[t, s:s + 1, :] * v_scr[s * dec_batch:(s + 1) * dec_batch, :]
            m_scr[t * dec_batch:(t + 1) * dec_batch, :] = acc
        vs_ref[...] = v_scr[...]
        back(rows, u)


def _mixer_a(x, layer, slot, g, w_in, ln_g, ln_b, w_s, bexp, wrow, brow, w_out, n_prompt, dec_batch, dec_seq):
    n = x.shape[0]
    assert n - n_prompt == ROW_TILE == dec_batch * dec_seq and dec_batch % SUBLANES == 0
    body = functools.partial(_mixa_body, n_prompt // ROW_TILE, dec_batch, dec_seq)
    return pl.pallas_call(
        body,
        grid=(n // ROW_TILE,),
        in_specs=[_row_spec(D_MODEL), _layer_spec((1, D_MODEL), layer), _layer_spec((D_MODEL, 2 * D_MODEL), slot),
                  _layer_spec((1, D_MODEL), slot), _layer_spec((1, D_MODEL), slot),
                  _layer_spec((A_GROUPS, CHUNK, CHUNK), slot), _const_spec((CHUNK, D_MODEL)),
                  _const_spec((dec_seq, dec_seq, D_MODEL)), _const_spec((dec_seq, D_MODEL)),
                  _layer_spec((D_MODEL, D_MODEL), slot)],
        out_specs=[_row_spec(D_MODEL), pl.BlockSpec((ROW_TILE, D_MODEL), lambda i: (0, 0))],
        out_shape=[jax.ShapeDtypeStruct((n, D_MODEL), F32),
                   jax.ShapeDtypeStruct((ROW_TILE, D_MODEL), F32)],
        scratch_shapes=[pltpu.VMEM((ROW_TILE, D_MODEL), F32), pltpu.VMEM((ROW_TILE, D_MODEL), F32)],
        compiler_params=_params(),
        name="mixer_a",
    )(x, g, w_in, ln_g, ln_b, w_s, bexp, wrow, brow, w_out)


HIST_PAD = 16


def _mixc_body(n_prompt_tiles, tiles_per_seq, dec_batch, dec_seq,
               x_ref, g_ref, win_ref, wgrp_ref, scale_ref, wout_ref, hist_ref,
               o_ref, tail_ref, us_ref, ext_scr, pool_scr):
    i = pl.program_id(0)

    def project(rows):
        return _dot(_rms(x_ref[rows, :], g_ref[...]).astype(BF16), win_ref[...])

    def finish(rows):
        parts = []
        for g in range(len(POOL_WINDOWS)):
            gs = slice(g * C_GROUP_DIM, (g + 1) * C_GROUP_DIM)
            parts.append(_dot(pool_scr[rows, gs].astype(BF16), wgrp_ref[g]))
        mixed = jnp.concatenate(parts, axis=1) * scale_ref[...]
        o_ref[rows, :] = x_ref[rows, :] + _dot(mixed.astype(BF16), wout_ref[...])

    @pl.when(i < n_prompt_tiles)
    def _():
        @pl.when(i % tiles_per_seq == 0)
        def _():
            ext_scr[0:HIST_PAD, :] = jnp.zeros((HIST_PAD, D_MODEL), F32)

        group_rows = ROW_TILE // MIX_ROW_GROUPS

        def pool(r):
            lo = r * group_rows
            pos = (i % tiles_per_seq) * ROW_TILE + lo + lax.broadcasted_iota(jnp.int32, (group_rows, 1), 0)
            for g, win in enumerate(POOL_WINDOWS):
                gs = slice(g * C_GROUP_DIM, (g + 1) * C_GROUP_DIM)
                cur = ext_scr[HIST_PAD + lo:HIST_PAD + lo + group_rows, gs]
                tot = cur
                for d in range(1, win):
                    tot = tot + ext_scr[HIST_PAD + lo - d:HIST_PAD + lo - d + group_rows, gs]
                inv = 1.0 / jnp.minimum(pos + 1, win).astype(F32)
                pool_scr[lo:lo + group_rows, gs] = tot * inv - cur

        for r in range(MIX_ROW_GROUPS):
            if r == 0:
                ext_scr[HIST_PAD:HIST_PAD + group_rows, :] = project(slice(0, group_rows))
            if r + 1 < MIX_ROW_GROUPS:
                nxt = slice((r + 1) * group_rows, (r + 2) * group_rows)
                ext_scr[HIST_PAD + nxt.start:HIST_PAD + nxt.stop, :] = project(nxt)
            pool(r)
            finish(slice(r * group_rows, (r + 1) * group_rows))
        tail = ext_scr[ROW_TILE:, :]
        tail_ref[0] = tail
        ext_scr[0:HIST_PAD, :] = tail

    @pl.when(i >= n_prompt_tiles)
    def _():
        u = project(slice(0, ROW_TILE))
        ext_scr[0:ROW_TILE, :] = u
        us_ref[...] = u
        tail_ref[0] = u[ROW_TILE - HIST_PAD:, :]

        def ext_row(j, gs):
            if j < POOL_HIST:
                return hist_ref[j, :, gs]
            return ext_scr[(j - POOL_HIST) * dec_batch:(j - POOL_HIST + 1) * dec_batch, gs]

        for t in range(dec_seq):
            for g, win in enumerate(POOL_WINDOWS):
                gs = slice(g * C_GROUP_DIM, (g + 1) * C_GROUP_DIM)
                cur = ext_row(POOL_HIST + t, gs)
                tot = cur
                for d in range(1, win):
                    tot = tot + ext_row(POOL_HIST + t - d, gs)
                pool_scr[t * dec_batch:(t + 1) * dec_batch, gs] = tot * (1.0 / win) - cur
        finish(slice(0, ROW_TILE))


def _mixer_c(x, layer, slot, g, w_in, w_grp, scale, w_out, hist_t, n_prompt, seq, dec_batch, dec_seq):
    n = x.shape[0]
    nt = n // ROW_TILE
    assert n - n_prompt == ROW_TILE == dec_batch * dec_seq and seq % ROW_TILE == 0
    assert POOL_HIST <= HIST_PAD and max(POOL_WINDOWS) - 1 <= POOL_HIST
    body = functools.partial(_mixc_body, n_prompt // ROW_TILE, seq // ROW_TILE, dec_batch, dec_seq)
    return pl.pallas_call(
        body,
        grid=(nt,),
        in_specs=[_row_spec(D_MODEL), _layer_spec((1, D_MODEL), layer), _layer_spec((D_MODEL, D_MODEL), slot),
                  _layer_spec((len(POOL_WINDOWS), C_GROUP_DIM, C_GROUP_DIM), slot), _layer_spec((1, D_MODEL), slot),
                  _layer_spec((D_MODEL, D_MODEL), slot), _const_spec((POOL_HIST, dec_batch, D_MODEL))],
        out_specs=[_row_spec(D_MODEL),
                   pl.BlockSpec((1, HIST_PAD, D_MODEL), lambda i: (i, 0, 0)),
                   pl.BlockSpec((ROW_TILE, D_MODEL), lambda i: (0, 0))],
        out_shape=[jax.ShapeDtypeStruct((n, D_MODEL), F32),
                   jax.ShapeDtypeStruct((nt, HIST_PAD, D_MODEL), F32),
                   jax.ShapeDtypeStruct((ROW_TILE, D_MODEL), F32)],
        scratch_shapes=[pltpu.VMEM((HIST_PAD + ROW_TILE, D_MODEL), F32),
                        pltpu.VMEM((ROW_TILE, D_MODEL), F32)],
        compiler_params=_params(),
        name="mixer_c",
    )(x, g, w_in, w_grp, scale, w_out, hist_t)


def _rope(xh, cos_t, sin_t, lane):
    partner = jnp.where(lane < ROT_DIM // 2, pltpu.roll(xh, LANES - ROT_DIM // 2, 1),
                        pltpu.roll(xh, ROT_DIM // 2, 1))
    return xh * cos_t + partner * sin_t


def _qkv_body(n_prompt_tiles, x_ref, g_ref, w_ref, cos_ref, sin_ref,
              q_ref, kp_ref, vp_ref, ks_ref, vs_ref):
    i = pl.program_id(0)
    group_rows = ROW_TILE // MIX_ROW_GROUPS
    groups = [slice(r * group_rows, (r + 1) * group_rows) for r in range(MIX_ROW_GROUPS)]
    lane = lax.broadcasted_iota(jnp.int32, (group_rows, LANES), 1)

    def project(rows):
        return _dot(_rms(x_ref[rows, :], g_ref[...]).astype(BF16), w_ref[...])

    def emit(k_ref, v_ref):
        def rotate(rows, qkv):
            cos_t = cos_ref[rows, :]
            sin_t = sin_ref[rows, :]
            for hd in range(N_HEADS):
                cols = slice(hd * HEAD_DIM, (hd + 1) * HEAD_DIM)
                q_ref[rows, cols] = _rope(qkv[:, cols], cos_t, sin_t, lane)
                off = D_MODEL + hd * HEAD_DIM
                k_ref[rows, cols] = _rope(qkv[:, off:off + HEAD_DIM], cos_t, sin_t, lane)
            v_ref[rows, :] = qkv[:, 2 * D_MODEL:]

        qkv = project(groups[0])
        for r in range(MIX_ROW_GROUPS):
            qkv_next = project(groups[r + 1]) if r + 1 < MIX_ROW_GROUPS else None
            rotate(groups[r], qkv)
            qkv = qkv_next

    pl.when(i < n_prompt_tiles)(functools.partial(emit, kp_ref, vp_ref))
    pl.when(i >= n_prompt_tiles)(functools.partial(emit, ks_ref, vs_ref))


def _qkv(x, layer, slot, g, w_qkv, cos_t, sin_t, n_prompt, seq):
    n = x.shape[0]
    npt = n_prompt // ROW_TILE
    tps = seq // ROW_TILE
    assert n - n_prompt == ROW_TILE and cos_t.shape[0] == seq + ROW_TILE
    prompt_spec = pl.BlockSpec((ROW_TILE, D_MODEL), lambda i: (jnp.minimum(i, npt - 1), 0))
    sample_spec = pl.BlockSpec((ROW_TILE, D_MODEL), lambda i: (0, 0))
    rope_spec = pl.BlockSpec((ROW_TILE, LANES), lambda i: (jnp.where(i < npt, i % tps, tps), 0))
    return pl.pallas_call(
        functools.partial(_qkv_body, npt),
        grid=(n // ROW_TILE,),
        in_specs=[_row_spec(D_MODEL), _layer_spec((1, D_MODEL), layer), _layer_spec((D_MODEL, 3 * D_MODEL), slot),
                  rope_spec, rope_spec],
        out_specs=[_row_spec(D_MODEL), prompt_spec, prompt_spec, sample_spec, sample_spec],
        out_shape=[jax.ShapeDtypeStruct((n, D_MODEL), F32),
                   jax.ShapeDtypeStruct((n_prompt, D_MODEL), F32),
                   jax.ShapeDtypeStruct((n_prompt, D_MODEL), F32),
                   jax.ShapeDtypeStruct((ROW_TILE, D_MODEL), F32),
                   jax.ShapeDtypeStruct((ROW_TILE, D_MODEL), F32)],
        compiler_params=_params(),
        name="moba_qkv",
    )(x, g, w_qkv, cos_t, sin_t)


def _top_k_mask(gate, valid, k, axis):
    idx = lax.broadcasted_iota(jnp.int32, gate.shape, axis)
    size = gate.shape[axis]
    g = jnp.where(valid, gate, -jnp.inf)
    chosen = jnp.zeros(gate.shape, dtype=jnp.bool_)
    for _ in range(k):
        mx = jnp.max(g, axis=axis, keepdims=True)
        first = jnp.min(jnp.where(g == mx, idx, size), axis=axis, keepdims=True)
        pick = (idx == first) & (mx > -jnp.inf)
        chosen = chosen | pick
        g = jnp.where(pick, -jnp.inf, g)
    return chosen


KV_CHUNK = 4
PREP_BLOCKS = 4
SUM_ROWS = 16


def _column_fold(x, op):
    return op(x.reshape(x.shape[0] // SUBLANES, SUBLANES, x.shape[1]), axis=0)


def _pattn_body(n_blocks, q_ref, k_ref, v_ref, o_ref, kaug_scr, vt_scr, qaug_scr, kmean_scr, s_scr):
    qp = pl.program_id(2)
    exp2_scale = (HEAD_DIM ** -0.5) * math.log2(math.e)
    chunk = KV_CHUNK * MOBA_BLOCK

    @pl.when(qp == 0)
    def _():
        lane = lax.broadcasted_iota(jnp.int32, (MOBA_BLOCK, LANES), 1)
        sum_rows = jnp.where(lax.broadcasted_iota(jnp.int32, (SUM_ROWS, MOBA_BLOCK), 0) == 0, 1.0, 0.0).astype(BF16)

        def fill(j, carry):
            rows = pl.ds(pl.multiple_of(j * MOBA_BLOCK, MOBA_BLOCK), MOBA_BLOCK)
            kj = k_ref[rows, :]
            kaug_scr[rows, 0:HEAD_DIM] = kj.astype(BF16)
            kaug_scr[rows, HEAD_DIM:] = jnp.where(lane == j, 1.0, 0.0).astype(BF16)
            vt_scr[j] = jnp.concatenate([v_ref[rows, :].T.astype(BF16), sum_rows], axis=0)
            kmean_scr[pl.ds(j, 1), :] = jnp.mean(kj, axis=0, keepdims=True)
            return carry

        lax.fori_loop(0, n_blocks, fill, 0, unroll=2)
        m_hi, m_lo = _split_bf16(kmean_scr[...])
        width = PREP_BLOCKS * MOBA_BLOCK
        blk = lax.broadcasted_iota(jnp.int32, (n_blocks, width), 0)
        col_blk = lax.broadcasted_iota(jnp.int32, (n_blocks, width), 1) // MOBA_BLOCK
        pad = jnp.zeros((LANES - n_blocks, MOBA_BLOCK), BF16)

        def prep(jj, carry):
            rows = pl.ds(pl.multiple_of(jj * width, width), width)
            qt = q_ref[rows, :].T
            qt_hi, qt_lo = _split_bf16(qt)
            gate = _dot(m_hi, qt_hi) + _dot(m_lo, qt_hi) + _dot(m_hi, qt_lo)
            allowed = _top_k_mask(gate, blk < col_blk + jj * PREP_BLOCKS, MOBA_TOP_K, 0)
            bias = jnp.where(allowed, 0.0, NEG_BIG).astype(BF16)
            qt_op = (qt * exp2_scale).astype(BF16)
            for i in range(PREP_BLOCKS):
                cols = slice(i * MOBA_BLOCK, (i + 1) * MOBA_BLOCK)
                qaug_scr[jj * PREP_BLOCKS + i] = jnp.concatenate([qt_op[:, cols], bias[:, cols], pad], axis=0)
            return carry

        lax.fori_loop(0, n_blocks // PREP_BLOCKS, prep, 0, unroll=2)

    blk_a = 2 * qp
    q_aug = jnp.concatenate([qaug_scr[blk_a], qaug_scr[blk_a + 1]], axis=1)
    own_a = pl.ds(pl.multiple_of(blk_a * MOBA_BLOCK, MOBA_BLOCK), MOBA_BLOCK)
    own_b = pl.ds(pl.multiple_of((blk_a + 1) * MOBA_BLOCK, MOBA_BLOCK), MOBA_BLOCK)

    def attend(k_chunks):
        def score(c):
            s_scr[c % 2] = _dot(kaug_scr[c * chunk:(c + 1) * chunk, :], q_aug)

        def values(c, pb):
            pv = _dot(vt_scr[c * KV_CHUNK], pb[0:MOBA_BLOCK, :])
            for j in range(1, KV_CHUNK):
                pv = pv + _dot(vt_scr[c * KV_CHUNK + j], pb[j * MOBA_BLOCK:(j + 1) * MOBA_BLOCK, :])
            return pv

        score(0)
        if k_chunks > 1:
            score(1)
        s_own = jnp.concatenate(
            [_dot(kaug_scr[own_a, 0:HEAD_DIM], q_aug[0:HEAD_DIM, 0:MOBA_BLOCK]),
             _dot(kaug_scr[own_b, 0:HEAD_DIM], q_aug[0:HEAD_DIM, MOBA_BLOCK:])], axis=1)
        key = lax.broadcasted_iota(jnp.int32, s_own.shape, 0)
        qry = lax.broadcasted_iota(jnp.int32, s_own.shape, 1) % MOBA_BLOCK
        s_own = jnp.where(key <= qry, s_own, NEG_BIG)
        s = s_scr[0]
        m8 = jnp.maximum(_column_fold(s_own, jnp.max), _column_fold(s, jnp.max))
        m = jnp.max(m8, axis=0, keepdims=True)
        pb_own = jnp.exp2(s_own - m).astype(BF16)
        acc = jnp.concatenate([_dot(vt_scr[blk_a], pb_own[:, 0:MOBA_BLOCK]),
                               _dot(vt_scr[blk_a + 1], pb_own[:, MOBA_BLOCK:])], axis=1)
        acc = acc + values(0, jnp.exp2(s - m).astype(BF16))
        for c in range(1, k_chunks):
            if c + 1 < k_chunks:
                score(c + 1)
            s = s_scr[c % 2]
            m_new = jnp.maximum(m, jnp.max(_column_fold(s, jnp.max), axis=0, keepdims=True))
            acc = jnp.exp2(m - m_new) * acc + values(c, jnp.exp2(s - m_new).astype(BF16))
            m = m_new
        o_ref[...] = (acc[0:HEAD_DIM, :] * (1.0 / acc[HEAD_DIM:HEAD_DIM + 1, :])).T.astype(o_ref.dtype)

    n_chunks = (blk_a + 1 + KV_CHUNK - 1) // KV_CHUNK
    for k_chunks in range(1, n_blocks // KV_CHUNK + 1):
        pl.when(n_chunks == k_chunks)(functools.partial(attend, k_chunks))


def _prompt_attention(q_all, k_p, v_p, batch, seq):
    n_blocks = seq // MOBA_BLOCK
    assert seq % MOBA_BLOCK == 0 and n_blocks <= LANES
    assert n_blocks % KV_CHUNK == 0 and n_blocks % PREP_BLOCKS == 0
    assert n_blocks % (2 * SUBLANES) == 0
    n_pairs = n_blocks // 2
    pairspec = pl.BlockSpec((2 * MOBA_BLOCK, HEAD_DIM), lambda b, h, i: (b * n_pairs + i, h))
    seqspec = pl.BlockSpec((seq, HEAD_DIM), lambda b, h, i: (b, h))
    return pl.pallas_call(
        functools.partial(_pattn_body, n_blocks),
        grid=(batch, N_HEADS, n_pairs),
        in_specs=[seqspec, seqspec, seqspec],
        out_specs=pairspec,
        out_shape=jax.ShapeDtypeStruct((batch * seq, D_MODEL), BF16),
        scratch_shapes=[pltpu.VMEM((seq, 2 * LANES), BF16),
                        pltpu.VMEM((n_blocks, HEAD_DIM + SUM_ROWS, MOBA_BLOCK), BF16),
                        pltpu.VMEM((n_blocks, 2 * LANES, MOBA_BLOCK), BF16),
                        pltpu.VMEM((n_blocks, HEAD_DIM), F32),
                        pltpu.VMEM((2, KV_CHUNK * MOBA_BLOCK, 2 * MOBA_BLOCK), F32)],
        compiler_params=_params(3),
        name="moba_prompt_attn",
    )(q_all, k_p, v_p)


def _sattn_body(n_pages, dec_seq, pt_ref, q_ref, kn_ref, vn_ref, ck_hbm, cv_hbm, o_ref,
                kbuf, vbuf, sem, s_scr, kmean_scr):
    n = pl.program_id(0)
    n_seq = pl.num_programs(0)
    slot = n % 2
    scale = HEAD_DIM ** -0.5
    pages_per_block = MOBA_BLOCK // PAGE_SIZE
    n_blocks = n_pages // pages_per_block

    def page_copy(hbm, buf, seq, sl, p, kind):
        return pltpu.make_async_copy(hbm.at[pt_ref[seq * n_pages + p]], buf.at[sl, p], sem.at[kind, sl])

    def start_all(seq, sl):
        for p in range(n_pages):
            page_copy(ck_hbm, kbuf, seq, sl, p, 0).start()
            page_copy(cv_hbm, vbuf, seq, sl, p, 1).start()

    @pl.when(n == 0)
    def _():
        start_all(0, 0)

    @pl.when(n + 1 < n_seq)
    def _():
        start_all(n + 1, 1 - slot)

    for p in range(n_pages):
        page_copy(ck_hbm, kbuf, n, slot, p, 0).wait()
        page_copy(cv_hbm, vbuf, n, slot, p, 1).wait()

    def page_rows(buf, p):
        return jnp.concatenate([buf[slot, p, pl.ds(hd, PAGE_SIZE, stride=N_HEADS), :]
                                for hd in range(N_HEADS)], axis=1)

    pad_rows = jnp.zeros((PAGE_SIZE - dec_seq, D_MODEL), F32)
    q_pad = jnp.concatenate([q_ref[0], pad_rows], axis=0)
    row_t = lax.broadcasted_iota(jnp.int32, (PAGE_SIZE, LANES), 0)
    col_c = lax.broadcasted_iota(jnp.int32, (PAGE_SIZE, LANES), 1)
    pick = jnp.where((col_c % dec_seq == row_t) & (col_c < N_HEADS * dec_seq), 1.0, 0.0).astype(BF16)
    head_r = lax.broadcasted_iota(jnp.int32, (D_MODEL, LANES), 0) // HEAD_DIM
    head_c = lax.broadcasted_iota(jnp.int32, (D_MODEL, LANES), 1) // dec_seq
    same_head = head_r == head_c
    q_hi, q_lo = _split_bf16(q_pad)
    qbd_hi = jnp.where(same_head, _dot_t0(q_hi, pick), 0.0).astype(BF16)
    qbd_lo = jnp.where(same_head, _dot_t0(q_lo, pick), 0.0).astype(BF16)

    kmean_scr[...] = jnp.zeros(kmean_scr.shape, F32)
    for b in range(n_blocks):
        ksum = None
        for r in range(pages_per_block):
            p = b * pages_per_block + r
            kp = page_rows(kbuf, p)
            s_scr[p * PAGE_SIZE:(p + 1) * PAGE_SIZE, :] = _dot(kp.astype(BF16), qbd_hi) * scale
            part = jnp.sum(kp, axis=0, keepdims=True)
            ksum = part if ksum is None else ksum + part
        kmean_scr[b:b + 1, :] = ksum * (1.0 / MOBA_BLOCK)
    m_hi, m_lo = _split_bf16(kmean_scr[...])
    gate = _dot(m_hi, qbd_hi) + _dot(m_lo, qbd_hi) + _dot(m_hi, qbd_lo)
    blk = lax.broadcasted_iota(jnp.int32, gate.shape, 0)
    allowed = _top_k_mask(gate, blk < n_blocks, min(MOBA_TOP_K, n_blocks), 0)
    bias = jnp.where(allowed, 0.0, NEG_BIG)

    k_pad = jnp.concatenate([kn_ref[0], pad_rows], axis=0)
    v_pad = jnp.concatenate([vn_ref[0], pad_rows], axis=0)
    s_new = _dot(k_pad.astype(BF16), qbd_hi) * scale
    s_new = jnp.where((row_t <= col_c % dec_seq) & (row_t < dec_seq), s_new, NEG_BIG)

    m = jnp.max(s_new, axis=0, keepdims=True)
    for b in range(n_blocks):
        rows = slice(b * MOBA_BLOCK, (b + 1) * MOBA_BLOCK)
        sb = s_scr[rows, :] + bias[b:b + 1, :]
        s_scr[rows, :] = sb
        m = jnp.maximum(m, jnp.max(sb, axis=0, keepdims=True))
    p_new = jnp.exp(s_new - m)
    l = jnp.sum(p_new, axis=0, keepdims=True)
    for p in range(n_pages):
        rows = slice(p * PAGE_SIZE, (p + 1) * PAGE_SIZE)
        pp = jnp.exp(s_scr[rows, :] - m)
        s_scr[rows, :] = pp
        l = l + jnp.sum(pp, axis=0, keepdims=True)
    inv = 1.0 / l
    used = slice(0, N_HEADS * dec_seq)
    out = _dot_t0((p_new * inv)[:, used].astype(BF16), v_pad.astype(BF16))
    for p in range(n_pages):
        rows = slice(p * PAGE_SIZE, (p + 1) * PAGE_SIZE)
        out = out + _dot_t0((s_scr[rows, :] * inv)[:, used].astype(BF16), page_rows(vbuf, p).astype(BF16))
    out_c = lax.broadcasted_iota(jnp.int32, out.shape, 0)
    out_h = lax.broadcasted_iota(jnp.int32, out.shape, 1) // HEAD_DIM
    for t in range(dec_seq):
        keep = (out_c % dec_seq == t) & (out_c // dec_seq == out_h)
        o_ref[0, t:t + 1, :] = jnp.sum(jnp.where(keep, out, 0.0), axis=0, keepdims=True)


def _sample_attention(page_table, q_s, k_s, v_s, cache_k, cache_v):
    n_seq, n_pages = page_table.shape
    dec_seq = q_s.shape[1]
    assert n_pages % (MOBA_BLOCK // PAGE_SIZE) == 0 and N_HEADS * dec_seq <= LANES and dec_seq <= SUBLANES
    row_spec = pl.BlockSpec((1, dec_seq, D_MODEL), lambda n, pt: (n, 0, 0))
    any_spec = pl.BlockSpec(memory_space=pl.ANY)
    grid_spec = pltpu.PrefetchScalarGridSpec(
        num_scalar_prefetch=1,
        grid=(n_seq,),
        in_specs=[row_spec, row_spec, row_spec, any_spec, any_spec],
        out_specs=row_spec,
        scratch_shapes=[pltpu.VMEM((2, n_pages, PAGE_SIZE * N_HEADS, HEAD_DIM), F32),
                        pltpu.VMEM((2, n_pages, PAGE_SIZE * N_HEADS, HEAD_DIM), F32),
                        pltpu.SemaphoreType.DMA((2, 2)),
                        pltpu.VMEM((n_pages * PAGE_SIZE, LANES), F32),
                        pltpu.VMEM((2 * SUBLANES, D_MODEL), F32)],
    )
    return pl.pallas_call(
        functools.partial(_sattn_body, n_pages, dec_seq),
        grid_spec=grid_spec,
        out_shape=jax.ShapeDtypeStruct((n_seq, dec_seq, D_MODEL), F32),
        compiler_params=_params(),
        name="moba_sample_attn",
    )(page_table.reshape(-1), q_s, k_s, v_s, cache_k, cache_v)


def _rope_tables(seq, past_len, dec_batch, dec_seq):
    half = ROT_DIM // 2
    pos = np.concatenate([np.arange(seq), past_len + np.repeat(np.arange(dec_seq), dec_batch)]).astype(np.float64)
    inv_freq = ROPE_THETA ** (-np.arange(half, dtype=np.float64) / half)
    ang = pos[:, None] * inv_freq[None, :]
    cos, sin = np.cos(ang), np.sin(ang)
    ones = np.ones((pos.shape[0], LANES - ROT_DIM))
    cos_t = np.concatenate([cos, cos, ones], axis=1).astype(np.float32)
    sin_t = np.concatenate([-sin, sin, 0.0 * ones], axis=1).astype(np.float32)
    return jnp.asarray(cos_t), jnp.asarray(sin_t)


def _to_time_major(a):
    return jnp.swapaxes(a, 0, 1).reshape((a.shape[0] * a.shape[1],) + a.shape[2:])


def _from_time_major(a, dec_batch, dec_seq):
    return jnp.swapaxes(a.reshape((dec_seq, dec_batch) + a.shape[1:]), 0, 1)


def kernel(x_prompt, x_sample, cache_k, cache_v, state_pool, page_table, p_prompt, p_sample, norm_ffn1, norm_mix, norm_ffn2, norm_ple, norm_final, ffn1_w_gate, ffn1_w_up, ffn1_w_down, ffn2_w_gate, ffn2_w_up, ffn2_w_down, ple_w_gate, ple_w_proj, a_w_in, a_ln_g, a_ln_b, a_w_s, a_b_s, a_w_out, b_w_qkv, b_w_out, c_w_in, c_w_grp, c_scale, c_w_out):
    batch, seq, d = x_prompt.shape
    dec_batch, dec_seq, _ = x_sample.shape
    n_prompt = batch * seq
    n_sample = dec_batch * dec_seq
    n_pages = page_table.shape[1]
    past_len = n_pages * PAGE_SIZE
    assert d == D_MODEL and n_sample == ROW_TILE and n_prompt % ROW_TILE == 0
    assert past_len % MOBA_BLOCK == 0

    x = (x_prompt.reshape(n_prompt, d), _to_time_major(x_sample))
    p_rows_prompt = p_prompt.reshape(DEPTH, n_prompt, PLE_DIM)
    p_rows_sample = jnp.swapaxes(p_sample, 1, 2).reshape(DEPTH, n_sample, PLE_DIM)
    rows = lambda v: v.reshape(v.shape[0], 1, v.shape[1])
    bf = lambda w: w.astype(BF16)
    g_ffn1, g_mix, g_ffn2, g_ple = rows(norm_ffn1), rows(norm_mix), rows(norm_ffn2), rows(norm_ple)
    w1 = (ffn1_w_gate, ffn1_w_up, ffn1_w_down)
    w2 = (ffn2_w_gate, ffn2_w_up, ffn2_w_down)
    weights = tuple(bf(w[0]) for w in w1)
    small = [ple_w_gate, ple_w_proj, a_w_in, a_w_out, b_w_qkv, b_w_out, c_w_in,
             c_w_grp.reshape(c_w_grp.shape[0], -1, C_GROUP_DIM), c_w_out]

    k_p, v_p, k_s, v_s, pool_p, pool_s, chunk_v = [], [], [], [], [], [], []
    for i in range(DEPTH):
        kind, j = LAYER_MIXERS[i], LAYER_SLOT[i]
        casts = [(w, i) for w in w2] + ([(w, None) for w in small] if i == 0 else [])
        x, narrowed = _ffn(x, i, g_ffn1, weights, n_prompt, cast_next=casts)
        weights = narrowed[:3]
        if i == 0:
            (ple_gate, ple_proj, a_in, a_out, w_qkv, w_attn_out, c_in, c_grp, c_out) = [
                nb.reshape(w.shape) for nb, w in zip(narrowed[3:], small)]
            ple = (g_ple, ple_gate, p_rows_prompt, p_rows_sample, ple_proj)
            a_w = (a_in, rows(a_ln_g), rows(a_ln_b), a_w_s)
            c_w = (c_in, c_grp.reshape(c_w_grp.shape), rows(c_scale), c_out)
        attn = None
        if kind == 0:
            bexp = jnp.repeat(a_b_s[j].T, LANES, axis=1)
            wrow = jnp.repeat(jnp.transpose(a_w_s[j][:, :dec_seq, :dec_seq], (1, 2, 0)), LANES, axis=2)
            brow = jnp.repeat(a_b_s[j][:, :dec_seq].T, LANES, axis=1)
            x, vs = _mixer_a(x, i, j, g_mix, *a_w, bexp, wrow, brow, a_out, n_prompt, dec_batch, dec_seq)
            chunk_v.append(_from_time_major(vs, dec_batch, dec_seq))
        elif kind == 1:
            cos_t, sin_t = _rope_tables(seq, past_len, dec_batch, dec_seq)
            q_all, kp, vp, ks, vs = _qkv(x, i, j, g_mix, w_qkv, cos_t, sin_t, n_prompt, seq)
            o_p = _prompt_attention(q_all, kp, vp, batch, seq)
            q_s = _from_time_major(q_all[n_prompt:], dec_batch, dec_seq)
            ks_n = _from_time_major(ks, dec_batch, dec_seq)
            vs_n = _from_time_major(vs, dec_batch, dec_seq)
            o_s = _sample_attention(page_table, q_s, ks_n, vs_n,
                                    cache_k[j].reshape(-1, PAGE_SIZE * N_HEADS, HEAD_DIM),
                                    cache_v[j].reshape(-1, PAGE_SIZE * N_HEADS, HEAD_DIM))
            attn = (o_p, _to_time_major(o_s).astype(BF16), w_attn_out, j)
            k_p.append(kp.reshape(batch, seq, N_HEADS, HEAD_DIM))
            v_p.append(vp.reshape(batch, seq, N_HEADS, HEAD_DIM))
            k_s.append(ks_n.reshape(dec_batch, dec_seq, N_HEADS, HEAD_DIM))
            v_s.append(vs_n.reshape(dec_batch, dec_seq, N_HEADS, HEAD_DIM))
        else:
            hist_t = jnp.swapaxes(state_pool[j], 0, 1)
            x, tails, us = _mixer_c(x, i, j, g_mix, *c_w, hist_t, n_prompt, seq, dec_batch, dec_seq)
            tps = seq // ROW_TILE
            pool_p.append(tails[tps - 1:batch * tps:tps, HIST_PAD - POOL_HIST:, :])
            u_new = _from_time_major(us, dec_batch, dec_seq)
            pool_s.append(jnp.concatenate([state_pool[j], u_new], axis=1)[:, -POOL_HIST:])
        last = i + 1 == DEPTH
        x, weights = _ffn(x, i, g_ffn2, weights, n_prompt, attn=attn, ple=ple,
                          final_gain=norm_final.reshape(1, -1) if last else None,
                          cast_next=None if last else [(w, i + 1) for w in w1])
    y_prompt, y_sample = x
    y_prompt = y_prompt.reshape(batch, seq, d)
    y_sample = _from_time_major(y_sample, dec_batch, dec_seq)
    return (y_prompt, y_sample, jnp.stack(k_p), jnp.stack(v_p), jnp.stack(k_s), jnp.stack(v_s),
            jnp.stack(pool_p), jnp.stack(pool_s), jnp.stack(chunk_v))
```

```python
import functools
import math

import jax
import jax.numpy as jnp
import numpy as np
from jax import lax
from jax.experimental import pallas as pl
from jax.experimental.pallas import tpu as pltpu

F32 = jnp.float32
BF16 = jnp.bfloat16

D_MODEL = 1024
DEPTH = 4
D_FF = 2816
PLE_DIM = 256
RMS_EPS = 1e-6
LN_EPS = 1e-5
CHUNK = 128
A_GROUPS = 8
N_HEADS = 8
HEAD_DIM = 128
ROT_DIM = 32
ROPE_THETA = 500000.0
MOBA_BLOCK = 256
MOBA_TOP_K = 3
PAGE_SIZE = 128
POOL_WINDOWS = (2, 4, 8, 16)
C_GROUP_DIM = 256
POOL_HIST = 15
LAYER_MIXERS = (0, 1, 2, 0)
LAYER_SLOT = (0, 0, 0, 1)

LANES = 128
SUBLANES = 8
VMEM_LIMIT = 56 * 1024 * 1024

ROW_TILE = 512
MIX_ROW_GROUPS = 2
FFN_ROW_GROUPS = 2
NEG_BIG = -1e30


def _dot(a, b):
    return jnp.dot(a, b, preferred_element_type=F32)


def _dot_t0(a, b):
    return lax.dot_general(a, b, (((0,), (0,)), ((), ())), preferred_element_type=F32)


def _dot_t1(a, b):
    return lax.dot_general(a, b, (((1,), (1,)), ((), ())), preferred_element_type=F32)


def _rms(x, g):
    return x * lax.rsqrt(jnp.mean(x * x, axis=-1, keepdims=True) + RMS_EPS) * g


def _split_bf16(x):
    hi = x.astype(BF16)
    lo = (x - hi.astype(F32)).astype(BF16)
    return hi, lo


def _const_spec(shape):
    nd = len(shape)
    return pl.BlockSpec(shape, lambda *_: (0,) * nd, pipeline_mode=pl.Buffered(1))


def _layer_spec(shape, layer):
    nd = len(shape)
    return pl.BlockSpec((None,) + tuple(shape), lambda *_: (layer,) + (0,) * nd, pipeline_mode=pl.Buffered(1))


def _row_spec(width):
    return pl.BlockSpec((ROW_TILE, width), lambda i: (i, 0))


def _params(n_axes=1):
    return pltpu.CompilerParams(dimension_semantics=("arbitrary",) * n_axes,
                                vmem_limit_bytes=VMEM_LIMIT)


def _ffn_body(n_prompt_tiles, split_x, attn, ple, final, n_cast, *refs):
    refs = list(refs)
    is_prompt = pl.program_id(0) < n_prompt_tiles
    take = lambda n: [refs.pop(0) for _ in range(n)]
    x_refs = take(2 if split_x else 1)
    attn_refs = take(3 if attn else 0)
    g_ref, wg_ref, wu_ref, wd_ref = take(4)
    ple_refs = take(5 if ple else 0)
    final_refs = take(1 if final else 0)
    cast_in = take(n_cast)
    out_refs = take(2 if final else 1)
    cast_out = refs

    for src, dst in zip(cast_in, cast_out):
        dst[...] = src[...].astype(BF16)

    def pick(prompt_ref, sample_ref, rows):
        return jnp.where(is_prompt, prompt_ref[rows, :], sample_ref[rows, :])

    def expand(rows):
        x = pick(*x_refs, rows) if split_x else x_refs[0][rows, :]
        if attn:
            op_ref, os_ref, wo_ref = attn_refs
            x = x + _dot(pick(op_ref, os_ref, rows), wo_ref[...])
        h = _rms(x, g_ref[...]).astype(BF16)
        return x, _dot(h, wg_ref[...]), _dot(h, wu_ref[...])

    def contract(x, gate, up):
        a = (gate * jax.nn.sigmoid(gate) * up).astype(BF16)
        return x + 0.5 * _dot(a, wd_ref[...])

    def embed(rows, x):
        if ple:
            gp_ref, wgate_ref, pp_ref, ps_ref, wproj_ref = ple_refs
            h = _rms(x, gp_ref[...]).astype(BF16)
            x = x + jax.nn.sigmoid(_dot(h, wgate_ref[...])) * _dot(pick(pp_ref, ps_ref, rows).astype(BF16),
                                                                    wproj_ref[...])
        return _rms(x, final_refs[0][...]) if final else x

    group_rows = ROW_TILE // FFN_ROW_GROUPS
    groups = [slice(r * group_rows, (r + 1) * group_rows) for r in range(FFN_ROW_GROUPS)]
    staged = [expand(rows) for rows in groups]
    staged = [contract(*s) for s in staged]
    staged = [embed(rows, x) for rows, x in zip(groups, staged)]
    if not final:
        for rows, x in zip(groups, staged):
            out_refs[0][rows, :] = x
        return
    yp_ref, ys_ref = out_refs

    @pl.when(is_prompt)
    def _():
        for rows, y in zip(groups, staged):
            yp_ref[rows, :] = y

    @pl.when(jnp.logical_not(is_prompt))
    def _():
        for rows, y in zip(groups, staged):
            ys_ref[rows, :] = y


WEIGHT_CAST_CHUNKS = 16


def _ffn(x, layer, gains, weights, n_prompt, attn=None, ple=None, final_gain=None, cast_next=None):
    wg, wu, wd = weights
    npt = n_prompt // ROW_TILE
    prompt_rows = lambda i: (jnp.minimum(i, npt - 1), 0)
    sample_rows = lambda i: (0, 0)
    split_x = isinstance(x, tuple)
    if split_x:
        args = list(x)
        in_specs = [pl.BlockSpec((ROW_TILE, D_MODEL), prompt_rows), pl.BlockSpec((ROW_TILE, D_MODEL), sample_rows)]
        n = n_prompt + x[1].shape[0]
    else:
        args = [x]
        in_specs = [_row_spec(D_MODEL)]
        n = x.shape[0]
    assert n - n_prompt == ROW_TILE
    if attn is not None:
        o_p, o_s, w_out, slot = attn
        args += [o_p, o_s, w_out]
        in_specs += [pl.BlockSpec((ROW_TILE, D_MODEL), prompt_rows), pl.BlockSpec((ROW_TILE, D_MODEL), sample_rows),
                     _layer_spec((D_MODEL, D_MODEL), slot)]
    args += [gains, wg, wu, wd]
    in_specs += [_layer_spec((1, D_MODEL), layer), _const_spec((D_MODEL, D_FF)),
                 _const_spec((D_MODEL, D_FF)), _const_spec((D_FF, D_MODEL))]
    if ple is not None:
        gp, wgate, p_prompt, p_sample, wproj = ple
        args += [gp, wgate, p_prompt, p_sample, wproj]
        in_specs += [_layer_spec((1, D_MODEL), layer), _layer_spec((D_MODEL, D_MODEL), layer),
                     pl.BlockSpec((None, ROW_TILE, PLE_DIM), lambda i: (layer, jnp.minimum(i, npt - 1), 0)),
                     pl.BlockSpec((None, ROW_TILE, PLE_DIM), lambda i: (layer, 0, 0)),
                     _layer_spec((PLE_DIM, D_MODEL), layer)]
    if final_gain is None:
        out_specs = [_row_spec(D_MODEL)]
        out_shape = [jax.ShapeDtypeStruct((n, D_MODEL), F32)]
    else:
        args.append(final_gain)
        in_specs.append(_const_spec((1, D_MODEL)))
        out_specs = [pl.BlockSpec((ROW_TILE, D_MODEL), prompt_rows), pl.BlockSpec((ROW_TILE, D_MODEL), sample_rows)]
        out_shape = [jax.ShapeDtypeStruct((n_prompt, D_MODEL), F32), jax.ShapeDtypeStruct((ROW_TILE, D_MODEL), F32)]
    n_main = len(out_shape)
    if cast_next:
        assert n // ROW_TILE >= WEIGHT_CAST_CHUNKS
        for w, layer_sel in cast_next:
            n_out = w.shape[1] if layer_sel is not None else w.shape[0] * w.shape[1]
            first = (layer_sel or 0) * WEIGHT_CAST_CHUNKS
            rows, cols = n_out // WEIGHT_CAST_CHUNKS, w.shape[2]
            assert rows * WEIGHT_CAST_CHUNKS == n_out and rows % (2 * SUBLANES) == 0
            chunk = lambda i: jnp.minimum(i, WEIGHT_CAST_CHUNKS - 1)
            args.append(w.reshape(-1, cols))
            in_specs.append(pl.BlockSpec((rows, cols), lambda i, chunk=chunk, first=first: (first + chunk(i), 0)))
            out_specs.append(pl.BlockSpec((rows, cols), lambda i, chunk=chunk: (chunk(i), 0)))
            out_shape.append(jax.ShapeDtypeStruct((n_out, cols), BF16))
    outs = pl.pallas_call(
        functools.partial(_ffn_body, npt, split_x, attn is not None, ple is not None, final_gain is not None,
                          len(cast_next) if cast_next else 0),
        grid=(n // ROW_TILE,),
        in_specs=in_specs,
        out_specs=out_specs,
        out_shape=out_shape,
        compiler_params=_params(),
        name="ffn_ple" if ple is not None else "ffn",
    )(*args)
    main = outs[0] if n_main == 1 else tuple(outs[:n_main])
    return main, tuple(outs[n_main:])


def _mixa_body(n_prompt_tiles, dec_batch, dec_seq,
               x_ref, g_ref, win_ref, lng_ref, lnb_ref, ws_ref, bexp_ref, wrow_ref, brow_ref, wout_ref,
               o_ref, vs_ref, v_scr, m_scr):
    i = pl.program_id(0)

    def project(rows):
        return _dot(_rms(x_ref[rows, :], g_ref[...]).astype(BF16), win_ref[...])

    def activate(rows, z):
        z = 0.5 * z * (1.0 + lax.erf(z * (1.0 / math.sqrt(2.0))))
        v = z[:, D_MODEL:]
        mu = jnp.mean(v, axis=-1, keepdims=True)
        vc = v - mu
        var = jnp.mean(vc * vc, axis=-1, keepdims=True)
        v_scr[rows, :] = vc * lax.rsqrt(var + LN_EPS) * lng_ref[...] + lnb_ref[...]
        return z[:, :D_MODEL]

    def back(rows, u):
        o_ref[rows, :] = x_ref[rows, :] + _dot((u * m_scr[rows, :]).astype(BF16), wout_ref[...])

    @pl.when(i < n_prompt_tiles)
    def _():
        r_iota = lax.broadcasted_iota(jnp.int32, (CHUNK, CHUNK), 0)
        c_iota = lax.broadcasted_iota(jnp.int32, (CHUNK, CHUNK), 1)
        w_tril = [jnp.where(c_iota <= r_iota, ws_ref[g], 0.0).astype(BF16) for g in range(A_GROUPS)]
        group_rows = ROW_TILE // MIX_ROW_GROUPS
        groups = [slice(r * group_rows, (r + 1) * group_rows) for r in range(MIX_ROW_GROUPS)]

        def mix(r):
            chunks = range(r * group_rows // CHUNK, (r + 1) * group_rows // CHUNK)
            for g in range(A_GROUPS):
                gs = slice(g * LANES, (g + 1) * LANES)
                rhs = jnp.concatenate([v_scr[c * CHUNK:(c + 1) * CHUNK, gs] for c in chunks], axis=1).astype(BF16)
                mg = _dot(w_tril[g], rhs)
                for k, c in enumerate(chunks):
                    m_scr[c * CHUNK:(c + 1) * CHUNK, gs] = mg[:, k * LANES:(k + 1) * LANES] + bexp_ref[:, gs]

        z = project(groups[0])
        for r in range(MIX_ROW_GROUPS):
            z_next = project(groups[r + 1]) if r + 1 < MIX_ROW_GROUPS else None
            u = activate(groups[r], z)
            mix(r)
            back(groups[r], u)
            z = z_next

    @pl.when(i >= n_prompt_tiles)
    def _():
        rows = slice(0, ROW_TILE)
        u = activate(rows, project(rows))
        for t in range(dec_seq):
            acc = brow_ref[t:t + 1, :] + wrow_ref[t, 0:1, :] * v_scr[0:dec_batch, :]
            for s in range(1, t + 1):
                acc = acc + wrow_ref[t, s:s + 1, :] * v_scr[s * dec_batch:(s + 1) * dec_batch, :]
            m_scr[t * dec_batch:(t + 1) * dec_batch, :] = acc
        vs_ref[...] = v_scr[...]
        back(rows, u)


def _mixer_a(x, layer, slot, g, w_in, ln_g, ln_b, w_s, bexp, wrow, brow, w_out, n_prompt, dec_batch, dec_seq):
    n = x.shape[0]
    assert n - n_prompt == ROW_TILE == dec_batch * dec_seq and dec_batch % SUBLANES == 0
    body = functools.partial(_mixa_body, n_prompt // ROW_TILE, dec_batch, dec_seq)
    return pl.pallas_call(
        body,
        grid=(n // ROW_TILE,),
        in_specs=[_row_spec(D_MODEL), _layer_spec((1, D_MODEL), layer), _layer_spec((D_MODEL, 2 * D_MODEL), slot),
                  _layer_spec((1, D_MODEL), slot), _layer_spec((1, D_MODEL), slot),
                  _layer_spec((A_GROUPS, CHUNK, CHUNK), slot), _const_spec((CHUNK, D_MODEL)),
                  _const_spec((dec_seq, dec_seq, D_MODEL)), _const_spec((dec_seq, D_MODEL)),
                  _layer_spec((D_MODEL, D_MODEL), slot)],
        out_specs=[_row_spec(D_MODEL), pl.BlockSpec((ROW_TILE, D_MODEL), lambda i: (0, 0))],
        out_shape=[jax.ShapeDtypeStruct((n, D_MODEL), F32),
                   jax.ShapeDtypeStruct((ROW_TILE, D_MODEL), F32)],
        scratch_shapes=[pltpu.VMEM((ROW_TILE, D_MODEL), F32), pltpu.VMEM((ROW_TILE, D_MODEL), F32)],
        compiler_params=_params(),
        name="mixer_a",
    )(x, g, w_in, ln_g, ln_b, w_s, bexp, wrow, brow, w_out)


HIST_PAD = 16


def _mixc_body(n_prompt_tiles, tiles_per_seq, dec_batch, dec_seq,
               x_ref, g_ref, win_ref, wgrp_ref, scale_ref, wout_ref, hist_ref,
               o_ref, tail_ref, us_ref, ext_scr, pool_scr):
    i = pl.program_id(0)

    def project(rows):
        return _dot(_rms(x_ref[rows, :], g_ref[...]).astype(BF16), win_ref[...])

    def finish(rows):
        parts = []
        for g in range(len(POOL_WINDOWS)):
            gs = slice(g * C_GROUP_DIM, (g + 1) * C_GROUP_DIM)
            parts.append(_dot(pool_scr[rows, gs].astype(BF16), wgrp_ref[g]))
        mixed = jnp.concatenate(parts, axis=1) * scale_ref[...]
        o_ref[rows, :] = x_ref[rows, :] + _dot(mixed.astype(BF16), wout_ref[...])

    @pl.when(i < n_prompt_tiles)
    def _():
        @pl.when(i % tiles_per_seq == 0)
        def _():
            ext_scr[0:HIST_PAD, :] = jnp.zeros((HIST_PAD, D_MODEL), F32)

        group_rows = ROW_TILE // MIX_ROW_GROUPS

        def pool(r):
            lo = r * group_rows
            pos = (i % tiles_per_seq) * ROW_TILE + lo + lax.broadcasted_iota(jnp.int32, (group_rows, 1), 0)
            for g, win in enumerate(POOL_WINDOWS):
                gs = slice(g * C_GROUP_DIM, (g + 1) * C_GROUP_DIM)
                cur = ext_scr[HIST_PAD + lo:HIST_PAD + lo + group_rows, gs]
                tot = cur
                for d in range(1, win):
                    tot = tot + ext_scr[HIST_PAD + lo - d:HIST_PAD + lo - d + group_rows, gs]
                inv = 1.0 / jnp.minimum(pos + 1, win).astype(F32)
                pool_scr[lo:lo + group_rows, gs] = tot * inv - cur

        for r in range(MIX_ROW_GROUPS):
            if r == 0:
                ext_scr[HIST_PAD:HIST_PAD + group_rows, :] = project(slice(0, group_rows))
            if r + 1 < MIX_ROW_GROUPS:
                nxt = slice((r + 1) * group_rows, (r + 2) * group_rows)
                ext_scr[HIST_PAD + nxt.start:HIST_PAD + nxt.stop, :] = project(nxt)
            pool(r)
            finish(slice(r * group_rows, (r + 1) * group_rows))
        tail = ext_scr[ROW_TILE:, :]
        tail_ref[0] = tail
        ext_scr[0:HIST_PAD, :] = tail

    @pl.when(i >= n_prompt_tiles)
    def _():
        u = project(slice(0, ROW_TILE))
        ext_scr[0:ROW_TILE, :] = u
        us_ref[...] = u
        tail_ref[0] = u[ROW_TILE - HIST_PAD:, :]

        def ext_row(j, gs):
            if j < POOL_HIST:
                return hist_ref[j, :, gs]
            return ext_scr[(j - POOL_HIST) * dec_batch:(j - POOL_HIST + 1) * dec_batch, gs]

        for t in range(dec_seq):
            for g, win in enumerate(POOL_WINDOWS):
                gs = slice(g * C_GROUP_DIM, (g + 1) * C_GROUP_DIM)
                cur = ext_row(POOL_HIST + t, gs)
                tot = cur
                for d in range(1, win):
                    tot = tot + ext_row(POOL_HIST + t - d, gs)
                pool_scr[t * dec_batch:(t + 1) * dec_batch, gs] = tot * (1.0 / win) - cur
        finish(slice(0, ROW_TILE))


def _mixer_c(x, layer, slot, g, w_in, w_grp, scale, w_out, hist_t, n_prompt, seq, dec_batch, dec_seq):
    n = x.shape[0]
    nt = n // ROW_TILE
    assert n - n_prompt == ROW_TILE == dec_batch * dec_seq and seq % ROW_TILE == 0
    assert POOL_HIST <= HIST_PAD and max(POOL_WINDOWS) - 1 <= POOL_HIST
    body = functools.partial(_mixc_body, n_prompt // ROW_TILE, seq // ROW_TILE, dec_batch, dec_seq)
    return pl.pallas_call(
        body,
        grid=(nt,),
        in_specs=[_row_spec(D_MODEL), _layer_spec((1, D_MODEL), layer), _layer_spec((D_MODEL, D_MODEL), slot),
                  _layer_spec((len(POOL_WINDOWS), C_GROUP_DIM, C_GROUP_DIM), slot), _layer_spec((1, D_MODEL), slot),
                  _layer_spec((D_MODEL, D_MODEL), slot), _const_spec((POOL_HIST, dec_batch, D_MODEL))],
        out_specs=[_row_spec(D_MODEL),
                   pl.BlockSpec((1, HIST_PAD, D_MODEL), lambda i: (i, 0, 0)),
                   pl.BlockSpec((ROW_TILE, D_MODEL), lambda i: (0, 0))],
        out_shape=[jax.ShapeDtypeStruct((n, D_MODEL), F32),
                   jax.ShapeDtypeStruct((nt, HIST_PAD, D_MODEL), F32),
                   jax.ShapeDtypeStruct((ROW_TILE, D_MODEL), F32)],
        scratch_shapes=[pltpu.VMEM((HIST_PAD + ROW_TILE, D_MODEL), F32),
                        pltpu.VMEM((ROW_TILE, D_MODEL), F32)],
        compiler_params=_params(),
        name="mixer_c",
    )(x, g, w_in, w_grp, scale, w_out, hist_t)


def _rope(xh, cos_t, sin_t, lane):
    partner = jnp.where(lane < ROT_DIM // 2, pltpu.roll(xh, LANES - ROT_DIM // 2, 1),
                        pltpu.roll(xh, ROT_DIM // 2, 1))
    return xh * cos_t + partner * sin_t


def _qkv_body(n_prompt_tiles, x_ref, g_ref, w_ref, cos_ref, sin_ref,
              q_ref, kp_ref, vp_ref, ks_ref, vs_ref):
    i = pl.program_id(0)
    group_rows = ROW_TILE // MIX_ROW_GROUPS
    groups = [slice(r * group_rows, (r + 1) * group_rows) for r in range(MIX_ROW_GROUPS)]
    lane = lax.broadcasted_iota(jnp.int32, (group_rows, LANES), 1)

    def project(rows):
        return _dot(_rms(x_ref[rows, :], g_ref[...]).astype(BF16), w_ref[...])

    def emit(k_ref, v_ref):
        def rotate(rows, qkv):
            cos_t = cos_ref[rows, :]
            sin_t = sin_ref[rows, :]
            for hd in range(N_HEADS):
                cols = slice(hd * HEAD_DIM, (hd + 1) * HEAD_DIM)
                q_ref[rows, cols] = _rope(qkv[:, cols], cos_t, sin_t, lane)
                off = D_MODEL + hd * HEAD_DIM
                k_ref[rows, cols] = _rope(qkv[:, off:off + HEAD_DIM], cos_t, sin_t, lane)
            v_ref[rows, :] = qkv[:, 2 * D_MODEL:]

        qkv = project(groups[0])
        for r in range(MIX_ROW_GROUPS):
            qkv_next = project(groups[r + 1]) if r + 1 < MIX_ROW_GROUPS else None
            rotate(groups[r], qkv)
            qkv = qkv_next

    pl.when(i < n_prompt_tiles)(functools.partial(emit, kp_ref, vp_ref))
    pl.when(i >= n_prompt_tiles)(functools.partial(emit, ks_ref, vs_ref))


def _qkv(x, layer, slot, g, w_qkv, cos_t, sin_t, n_prompt, seq):
    n = x.shape[0]
    npt = n_prompt // ROW_TILE
    tps = seq // ROW_TILE
    assert n - n_prompt == ROW_TILE and cos_t.shape[0] == seq + ROW_TILE
    prompt_spec = pl.BlockSpec((ROW_TILE, D_MODEL), lambda i: (jnp.minimum(i, npt - 1), 0))
    sample_spec = pl.BlockSpec((ROW_TILE, D_MODEL), lambda i: (0, 0))
    rope_spec = pl.BlockSpec((ROW_TILE, LANES), lambda i: (jnp.where(i < npt, i % tps, tps), 0))
    return pl.pallas_call(
        functools.partial(_qkv_body, npt),
        grid=(n // ROW_TILE,),
        in_specs=[_row_spec(D_MODEL), _layer_spec((1, D_MODEL), layer), _layer_spec((D_MODEL, 3 * D_MODEL), slot),
                  rope_spec, rope_spec],
        out_specs=[_row_spec(D_MODEL), prompt_spec, prompt_spec, sample_spec, sample_spec],
        out_shape=[jax.ShapeDtypeStruct((n, D_MODEL), F32),
                   jax.ShapeDtypeStruct((n_prompt, D_MODEL), F32),
                   jax.ShapeDtypeStruct((n_prompt, D_MODEL), F32),
                   jax.ShapeDtypeStruct((ROW_TILE, D_MODEL), F32),
                   jax.ShapeDtypeStruct((ROW_TILE, D_MODEL), F32)],
        compiler_params=_params(),
        name="moba_qkv",
    )(x, g, w_qkv, cos_t, sin_t)


def _top_k_mask(gate, valid, k, axis):
    idx = lax.broadcasted_iota(jnp.int32, gate.shape, axis)
    size = gate.shape[axis]
    g = jnp.where(valid, gate, -jnp.inf)
    chosen = jnp.zeros(gate.shape, dtype=jnp.bool_)
    for _ in range(k):
        mx = jnp.max(g, axis=axis, keepdims=True)
        first = jnp.min(jnp.where(g == mx, idx, size), axis=axis, keepdims=True)
        pick = (idx == first) & (mx > -jnp.inf)
        chosen = chosen | pick
        g = jnp.where(pick, -jnp.inf, g)
    return chosen


KV_CHUNK = 4
PREP_BLOCKS = 4
SUM_ROWS = 16


def _column_fold(x, op):
    return op(x.reshape(x.shape[0] // SUBLANES, SUBLANES, x.shape[1]), axis=0)


def _pattn_body(n_blocks, q_ref, k_ref, v_ref, o_ref, kaug_scr, vt_scr, qaug_scr, kmean_scr, s_scr):
    qp = pl.program_id(2)
    exp2_scale = (HEAD_DIM ** -0.5) * math.log2(math.e)
    chunk = KV_CHUNK * MOBA_BLOCK

    @pl.when(qp == 0)
    def _():
        lane = lax.broadcasted_iota(jnp.int32, (MOBA_BLOCK, LANES), 1)
        sum_rows = jnp.where(lax.broadcasted_iota(jnp.int32, (SUM_ROWS, MOBA_BLOCK), 0) == 0, 1.0, 0.0).astype(BF16)

        def fill(j, carry):
            rows = pl.ds(pl.multiple_of(j * MOBA_BLOCK, MOBA_BLOCK), MOBA_BLOCK)
            kj = k_ref[rows, :]
            kaug_scr[rows, 0:HEAD_DIM] = kj.astype(BF16)
            kaug_scr[rows, HEAD_DIM:] = jnp.where(lane == j, 1.0, 0.0).astype(BF16)
            vt_scr[j] = jnp.concatenate([v_ref[rows, :].T.astype(BF16), sum_rows], axis=0)
            kmean_scr[pl.ds(j, 1), :] = jnp.mean(kj, axis=0, keepdims=True)
            return carry

        lax.fori_loop(0, n_blocks, fill, 0, unroll=4)
        m_hi, m_lo = _split_bf16(kmean_scr[...])
        width = PREP_BLOCKS * MOBA_BLOCK
        blk = lax.broadcasted_iota(jnp.int32, (n_blocks, width), 0)
        col_blk = lax.broadcasted_iota(jnp.int32, (n_blocks, width), 1) // MOBA_BLOCK
        pad = jnp.zeros((LANES - n_blocks, MOBA_BLOCK), BF16)

        def prep(jj, carry):
            rows = pl.ds(pl.multiple_of(jj * width, width), width)
            qt = q_ref[rows, :].T
            qt_hi, qt_lo = _split_bf16(qt)
            gate = _dot(m_hi, qt_hi) + _dot(m_lo, qt_hi) + _dot(m_hi, qt_lo)
            allowed = _top_k_mask(gate, blk < col_blk + jj * PREP_BLOCKS, MOBA_TOP_K, 0)
            bias = jnp.where(allowed, 0.0, NEG_BIG).astype(BF16)
            qt_op = (qt * exp2_scale).astype(BF16)
            for i in range(PREP_BLOCKS):
                cols = slice(i * MOBA_BLOCK, (i + 1) * MOBA_BLOCK)
                qaug_scr[jj * PREP_BLOCKS + i] = jnp.concatenate([qt_op[:, cols], bias[:, cols], pad], axis=0)
            return carry

        lax.fori_loop(0, n_blocks // PREP_BLOCKS, prep, 0, unroll=2)

    blk_a = 2 * qp
    q_aug = jnp.concatenate([qaug_scr[blk_a], qaug_scr[blk_a + 1]], axis=1)
    own_a = pl.ds(pl.multiple_of(blk_a * MOBA_BLOCK, MOBA_BLOCK), MOBA_BLOCK)
    own_b = pl.ds(pl.multiple_of((blk_a + 1) * MOBA_BLOCK, MOBA_BLOCK), MOBA_BLOCK)

    def attend(k_chunks):
        def score(c):
            s_scr[c % 2] = _dot(kaug_scr[c * chunk:(c + 1) * chunk, :], q_aug)

        def values(c, pb):
            pv = _dot(vt_scr[c * KV_CHUNK], pb[0:MOBA_BLOCK, :])
            for j in range(1, KV_CHUNK):
                pv = pv + _dot(vt_scr[c * KV_CHUNK + j], pb[j * MOBA_BLOCK:(j + 1) * MOBA_BLOCK, :])
            return pv

        score(0)
        if k_chunks > 1:
            score(1)
        s_own = jnp.concatenate(
            [_dot(kaug_scr[own_a, 0:HEAD_DIM], q_aug[0:HEAD_DIM, 0:MOBA_BLOCK]),
             _dot(kaug_scr[own_b, 0:HEAD_DIM], q_aug[0:HEAD_DIM, MOBA_BLOCK:])], axis=1)
        key = lax.broadcasted_iota(jnp.int32, s_own.shape, 0)
        qry = lax.broadcasted_iota(jnp.int32, s_own.shape, 1) % MOBA_BLOCK
        s_own = jnp.where(key <= qry, s_own, NEG_BIG)
        s = s_scr[0]
        m8 = jnp.maximum(_column_fold(s_own, jnp.max), _column_fold(s, jnp.max))
        m = jnp.max(m8, axis=0, keepdims=True)
        pb_own = jnp.exp2(s_own - m).astype(BF16)
        acc = jnp.concatenate([_dot(vt_scr[blk_a], pb_own[:, 0:MOBA_BLOCK]),
                               _dot(vt_scr[blk_a + 1], pb_own[:, MOBA_BLOCK:])], axis=1)
        acc = acc + values(0, jnp.exp2(s - m).astype(BF16))
        for c in range(1, k_chunks):
            if c + 1 < k_chunks:
                score(c + 1)
            s = s_scr[c % 2]
            m_new = jnp.maximum(m, jnp.max(_column_fold(s, jnp.max), axis=0, keepdims=True))
            acc = jnp.exp2(m - m_new) * acc + values(c, jnp.exp2(s - m_new).astype(BF16))
            m = m_new
        o_ref[...] = (acc[0:HEAD_DIM, :] * (1.0 / acc[HEAD_DIM:HEAD_DIM + 1, :])).T.astype(o_ref.dtype)

    n_chunks = (blk_a + 1 + KV_CHUNK - 1) // KV_CHUNK
    for k_chunks in range(1, n_blocks // KV_CHUNK + 1):
        pl.when(n_chunks == k_chunks)(functools.partial(attend, k_chunks))


def _prompt_attention(q_all, k_p, v_p, batch, seq):
    n_blocks = seq // MOBA_BLOCK
    assert seq % MOBA_BLOCK == 0 and n_blocks <= LANES
    assert n_blocks % KV_CHUNK == 0 and n_blocks % PREP_BLOCKS == 0
    assert n_blocks % (2 * SUBLANES) == 0
    n_pairs = n_blocks // 2
    pairspec = pl.BlockSpec((2 * MOBA_BLOCK, HEAD_DIM), lambda b, h, i: (b * n_pairs + i, h))
    seqspec = pl.BlockSpec((seq, HEAD_DIM), lambda b, h, i: (b, h))
    return pl.pallas_call(
        functools.partial(_pattn_body, n_blocks),
        grid=(batch, N_HEADS, n_pairs),
        in_specs=[seqspec, seqspec, seqspec],
        out_specs=pairspec,
        out_shape=jax.ShapeDtypeStruct((batch * seq, D_MODEL), BF16),
        scratch_shapes=[pltpu.VMEM((seq, 2 * LANES), BF16),
                        pltpu.VMEM((n_blocks, HEAD_DIM + SUM_ROWS, MOBA_BLOCK), BF16),
                        pltpu.VMEM((n_blocks, 2 * LANES, MOBA_BLOCK), BF16),
                        pltpu.VMEM((n_blocks, HEAD_DIM), F32),
                        pltpu.VMEM((2, KV_CHUNK * MOBA_BLOCK, 2 * MOBA_BLOCK), F32)],
        compiler_params=_params(3),
        name="moba_prompt_attn",
    )(q_all, k_p, v_p)


def _sattn_body(n_pages, dec_seq, pt_ref, q_ref, kn_ref, vn_ref, ck_hbm, cv_hbm, o_ref,
                kbuf, vbuf, sem, s_scr, kmean_scr):
    n = pl.program_id(0)
    n_seq = pl.num_programs(0)
    slot = n % 2
    scale = HEAD_DIM ** -0.5
    pages_per_block = MOBA_BLOCK // PAGE_SIZE
    n_blocks = n_pages // pages_per_block

    def page_copy(hbm, buf, seq, sl, p, kind):
        return pltpu.make_async_copy(hbm.at[pt_ref[seq * n_pages + p]], buf.at[sl, p], sem.at[kind, sl])

    def start_all(seq, sl):
        for p in range(n_pages):
            page_copy(ck_hbm, kbuf, seq, sl, p, 0).start()
            page_copy(cv_hbm, vbuf, seq, sl, p, 1).start()

    @pl.when(n == 0)
    def _():
        start_all(0, 0)

    @pl.when(n + 1 < n_seq)
    def _():
        start_all(n + 1, 1 - slot)

    for p in range(n_pages):
        page_copy(ck_hbm, kbuf, n, slot, p, 0).wait()
        page_copy(cv_hbm, vbuf, n, slot, p, 1).wait()

    def page_rows(buf, p):
        return jnp.concatenate([buf[slot, p, pl.ds(hd, PAGE_SIZE, stride=N_HEADS), :]
                                for hd in range(N_HEADS)], axis=1)

    pad_rows = jnp.zeros((PAGE_SIZE - dec_seq, D_MODEL), F32)
    q_pad = jnp.concatenate([q_ref[0], pad_rows], axis=0)
    row_t = lax.broadcasted_iota(jnp.int32, (PAGE_SIZE, LANES), 0)
    col_c = lax.broadcasted_iota(jnp.int32, (PAGE_SIZE, LANES), 1)
    pick = jnp.where((col_c % dec_seq == row_t) & (col_c < N_HEADS * dec_seq), 1.0, 0.0).astype(BF16)
    head_r = lax.broadcasted_iota(jnp.int32, (D_MODEL, LANES), 0) // HEAD_DIM
    head_c = lax.broadcasted_iota(jnp.int32, (D_MODEL, LANES), 1) // dec_seq
    same_head = head_r == head_c
    q_hi, q_lo = _split_bf16(q_pad)
    qbd_hi = jnp.where(same_head, _dot_t0(q_hi, pick), 0.0).astype(BF16)
    qbd_lo = jnp.where(same_head, _dot_t0(q_lo, pick), 0.0).astype(BF16)

    kmean_scr[...] = jnp.zeros(kmean_scr.shape, F32)
    for b in range(n_blocks):
        ksum = None
        for r in range(pages_per_block):
            p = b * pages_per_block + r
            kp = page_rows(kbuf, p)
            s_scr[p * PAGE_SIZE:(p + 1) * PAGE_SIZE, :] = _dot(kp.astype(BF16), qbd_hi) * scale
            part = jnp.sum(kp, axis=0, keepdims=True)
            ksum = part if ksum is None else ksum + part
        kmean_scr[b:b + 1, :] = ksum * (1.0 / MOBA_BLOCK)
    m_hi, m_lo = _split_bf16(kmean_scr[...])
    gate = _dot(m_hi, qbd_hi) + _dot(m_lo, qbd_hi) + _dot(m_hi, qbd_lo)
    blk = lax.broadcasted_iota(jnp.int32, gate.shape, 0)
    allowed = _top_k_mask(gate, blk < n_blocks, min(MOBA_TOP_K, n_blocks), 0)
    bias = jnp.where(allowed, 0.0, NEG_BIG)

    k_pad = jnp.concatenate([kn_ref[0], pad_rows], axis=0)
    v_pad = jnp.concatenate([vn_ref[0], pad_rows], axis=0)
    s_new = _dot(k_pad.astype(BF16), qbd_hi) * scale
    s_new = jnp.where((row_t <= col_c % dec_seq) & (row_t < dec_seq), s_new, NEG_BIG)

    m = jnp.max(s_new, axis=0, keepdims=True)
    for b in range(n_blocks):
        rows = slice(b * MOBA_BLOCK, (b + 1) * MOBA_BLOCK)
        sb = s_scr[rows, :] + bias[b:b + 1, :]
        s_scr[rows, :] = sb
        m = jnp.maximum(m, jnp.max(sb, axis=0, keepdims=True))
    p_new = jnp.exp(s_new - m)
    l = jnp.sum(p_new, axis=0, keepdims=True)
    for p in range(n_pages):
        rows = slice(p * PAGE_SIZE, (p + 1) * PAGE_SIZE)
        pp = jnp.exp(s_scr[rows, :] - m)
        s_scr[rows, :] = pp
        l = l + jnp.sum(pp, axis=0, keepdims=True)
    inv = 1.0 / l
    used = slice(0, N_HEADS * dec_seq)
    out = _dot_t0((p_new * inv)[:, used].astype(BF16), v_pad.astype(BF16))
    for p in range(n_pages):
        rows = slice(p * PAGE_SIZE, (p + 1) * PAGE_SIZE)
        out = out + _dot_t0((s_scr[rows, :] * inv)[:, used].astype(BF16), page_rows(vbuf, p).astype(BF16))
    out_c = lax.broadcasted_iota(jnp.int32, out.shape, 0)
    out_h = lax.broadcasted_iota(jnp.int32, out.shape, 1) // HEAD_DIM
    for t in range(dec_seq):
        keep = (out_c % dec_seq == t) & (out_c // dec_seq == out_h)
        o_ref[0, t:t + 1, :] = jnp.sum(jnp.where(keep, out, 0.0), axis=0, keepdims=True)


def _sample_attention(page_table, q_s, k_s, v_s, cache_k, cache_v):
    n_seq, n_pages = page_table.shape
    dec_seq = q_s.shape[1]
    assert n_pages % (MOBA_BLOCK // PAGE_SIZE) == 0 and N_HEADS * dec_seq <= LANES and dec_seq <= SUBLANES
    row_spec = pl.BlockSpec((1, dec_seq, D_MODEL), lambda n, pt: (n, 0, 0))
    any_spec = pl.BlockSpec(memory_space=pl.ANY)
    grid_spec = pltpu.PrefetchScalarGridSpec(
        num_scalar_prefetch=1,
        grid=(n_seq,),
        in_specs=[row_spec, row_spec, row_spec, any_spec, any_spec],
        out_specs=row_spec,
        scratch_shapes=[pltpu.VMEM((2, n_pages, PAGE_SIZE * N_HEADS, HEAD_DIM), F32),
                        pltpu.VMEM((2, n_pages, PAGE_SIZE * N_HEADS, HEAD_DIM), F32),
                        pltpu.SemaphoreType.DMA((2, 2)),
                        pltpu.VMEM((n_pages * PAGE_SIZE, LANES), F32),
                        pltpu.VMEM((2 * SUBLANES, D_MODEL), F32)],
    )
    return pl.pallas_call(
        functools.partial(_sattn_body, n_pages, dec_seq),
        grid_spec=grid_spec,
        out_shape=jax.ShapeDtypeStruct((n_seq, dec_seq, D_MODEL), F32),
        compiler_params=_params(),
        name="moba_sample_attn",
    )(page_table.reshape(-1), q_s, k_s, v_s, cache_k, cache_v)


def _rope_tables(seq, past_len, dec_batch, dec_seq):
    half = ROT_DIM // 2
    pos = np.concatenate([np.arange(seq), past_len + np.repeat(np.arange(dec_seq), dec_batch)]).astype(np.float64)
    inv_freq = ROPE_THETA ** (-np.arange(half, dtype=np.float64) / half)
    ang = pos[:, None] * inv_freq[None, :]
    cos, sin = np.cos(ang), np.sin(ang)
    ones = np.ones((pos.shape[0], LANES - ROT_DIM))
    cos_t = np.concatenate([cos, cos, ones], axis=1).astype(np.float32)
    sin_t = np.concatenate([-sin, sin, 0.0 * ones], axis=1).astype(np.float32)
    return jnp.asarray(cos_t), jnp.asarray(sin_t)


def _to_time_major(a):
    return jnp.swapaxes(a, 0, 1).reshape((a.shape[0] * a.shape[1],) + a.shape[2:])


def _from_time_major(a, dec_batch, dec_seq):
    return jnp.swapaxes(a.reshape((dec_seq, dec_batch) + a.shape[1:]), 0, 1)


def kernel(x_prompt, x_sample, cache_k, cache_v, state_pool, page_table, p_prompt, p_sample, norm_ffn1, norm_mix, norm_ffn2, norm_ple, norm_final, ffn1_w_gate, ffn1_w_up, ffn1_w_down, ffn2_w_gate, ffn2_w_up, ffn2_w_down, ple_w_gate, ple_w_proj, a_w_in, a_ln_g, a_ln_b, a_w_s, a_b_s, a_w_out, b_w_qkv, b_w_out, c_w_in, c_w_grp, c_scale, c_w_out):
    batch, seq, d = x_prompt.shape
    dec_batch, dec_seq, _ = x_sample.shape
    n_prompt = batch * seq
    n_sample = dec_batch * dec_seq
    n_pages = page_table.shape[1]
    past_len = n_pages * PAGE_SIZE
    assert d == D_MODEL and n_sample == ROW_TILE and n_prompt % ROW_TILE == 0
    assert past_len % MOBA_BLOCK == 0

    x = (x_prompt.reshape(n_prompt, d), _to_time_major(x_sample))
    p_rows_prompt = p_prompt.reshape(DEPTH, n_prompt, PLE_DIM)
    p_rows_sample = jnp.swapaxes(p_sample, 1, 2).reshape(DEPTH, n_sample, PLE_DIM)
    rows = lambda v: v.reshape(v.shape[0], 1, v.shape[1])
    bf = lambda w: w.astype(BF16)
    g_ffn1, g_mix, g_ffn2, g_ple = rows(norm_ffn1), rows(norm_mix), rows(norm_ffn2), rows(norm_ple)
    w1 = (ffn1_w_gate, ffn1_w_up, ffn1_w_down)
    w2 = (ffn2_w_gate, ffn2_w_up, ffn2_w_down)
    weights = tuple(bf(w[0]) for w in w1)
    small = [ple_w_gate, ple_w_proj, a_w_in, a_w_out, b_w_qkv, b_w_out, c_w_in,
             c_w_grp.reshape(c_w_grp.shape[0], -1, C_GROUP_DIM), c_w_out]

    k_p, v_p, k_s, v_s, pool_p, pool_s, chunk_v = [], [], [], [], [], [], []
    for i in range(DEPTH):
        kind, j = LAYER_MIXERS[i], LAYER_SLOT[i]
        casts = [(w, i) for w in w2] + ([(w, None) for w in small] if i == 0 else [])
        x, narrowed = _ffn(x, i, g_ffn1, weights, n_prompt, cast_next=casts)
        weights = narrowed[:3]
        if i == 0:
            (ple_gate, ple_proj, a_in, a_out, w_qkv, w_attn_out, c_in, c_grp, c_out) = [
                nb.reshape(w.shape) for nb, w in zip(narrowed[3:], small)]
            ple = (g_ple, ple_gate, p_rows_prompt, p_rows_sample, ple_proj)
            a_w = (a_in, rows(a_ln_g), rows(a_ln_b), a_w_s)
            c_w = (c_in, c_grp.reshape(c_w_grp.shape), rows(c_scale), c_out)
        attn = None
        if kind == 0:
            bexp = jnp.repeat(a_b_s[j].T, LANES, axis=1)
            wrow = jnp.repeat(jnp.transpose(a_w_s[j][:, :dec_seq, :dec_seq], (1, 2, 0)), LANES, axis=2)
            brow = jnp.repeat(a_b_s[j][:, :dec_seq].T, LANES, axis=1)
            x, vs = _mixer_a(x, i, j, g_mix, *a_w, bexp, wrow, brow, a_out, n_prompt, dec_batch, dec_seq)
            chunk_v.append(_from_time_major(vs, dec_batch, dec_seq))
        elif kind == 1:
            cos_t, sin_t = _rope_tables(seq, past_len, dec_batch, dec_seq)
            q_all, kp, vp, ks, vs = _qkv(x, i, j, g_mix, w_qkv, cos_t, sin_t, n_prompt, seq)
            o_p = _prompt_attention(q_all, kp, vp, batch, seq)
            q_s = _from_time_major(q_all[n_prompt:], dec_batch, dec_seq)
            ks_n = _from_time_major(ks, dec_batch, dec_seq)
            vs_n = _from_time_major(vs, dec_batch, dec_seq)
            o_s = _sample_attention(page_table, q_s, ks_n, vs_n,
                                    cache_k[j].reshape(-1, PAGE_SIZE * N_HEADS, HEAD_DIM),
                                    cache_v[j].reshape(-1, PAGE_SIZE * N_HEADS, HEAD_DIM))
            attn = (o_p, _to_time_major(o_s).astype(BF16), w_attn_out, j)
            k_p.append(kp.reshape(batch, seq, N_HEADS, HEAD_DIM))
            v_p.append(vp.reshape(batch, seq, N_HEADS, HEAD_DIM))
            k_s.append(ks_n.reshape(dec_batch, dec_seq, N_HEADS, HEAD_DIM))
            v_s.append(vs_n.reshape(dec_batch, dec_seq, N_HEADS, HEAD_DIM))
        else:
            hist_t = jnp.swapaxes(state_pool[j], 0, 1)
            x, tails, us = _mixer_c(x, i, j, g_mix, *c_w, hist_t, n_prompt, seq, dec_batch, dec_seq)
            tps = seq // ROW_TILE
            pool_p.append(tails[tps - 1:batch * tps:tps, HIST_PAD - POOL_HIST:, :])
            u_new = _from_time_major(us, dec_batch, dec_seq)
            pool_s.append(jnp.concatenate([state_pool[j], u_new], axis=1)[:, -POOL_HIST:])
        last = i + 1 == DEPTH
        x, weights = _ffn(x, i, g_ffn2, weights, n_prompt, attn=attn, ple=ple,
                          final_gain=norm_final.reshape(1, -1) if last else None,
                          cast_next=None if last else [(w, i + 1) for w in w1])
    y_prompt, y_sample = x
    y_prompt = y_prompt.reshape(batch, seq, d)
    y_sample = _from_time_major(y_sample, dec_batch, dec_seq)
    return (y_prompt, y_sample, jnp.stack(k_p), jnp.stack(v_p), jnp.stack(k_s), jnp.stack(v_s),
            jnp.stack(pool_p), jnp.stack(pool_s), jnp.stack(chunk_v))
```

```python
import functools
import math

import jax
import jax.numpy as jnp
import numpy as np
from jax import lax
from jax.experimental import pallas as pl
from jax.experimental.pallas import tpu as pltpu

F32 = jnp.float32
BF16 = jnp.bfloat16

D_MODEL = 1024
DEPTH = 4
D_FF = 2816
PLE_DIM = 256
RMS_EPS = 1e-6
LN_EPS = 1e-5
CHUNK = 128
A_GROUPS = 8
N_HEADS = 8
HEAD_DIM = 128
ROT_DIM = 32
ROPE_THETA = 500000.0
MOBA_BLOCK = 256
MOBA_TOP_K = 3
PAGE_SIZE = 128
POOL_WINDOWS = (2, 4, 8, 16)
C_GROUP_DIM = 256
POOL_HIST = 15
LAYER_MIXERS = (0, 1, 2, 0)
LAYER_SLOT = (0, 0, 0, 1)

LANES = 128
SUBLANES = 8
VMEM_LIMIT = 56 * 1024 * 1024

ROW_TILE = 512
MIX_ROW_GROUPS = 2
FFN_ROW_GROUPS = 2
NEG_BIG = -1e30


def _dot(a, b):
    return jnp.dot(a, b, preferred_element_type=F32)


def _dot_t0(a, b):
    return lax.dot_general(a, b, (((0,), (0,)), ((), ())), preferred_element_type=F32)


def _rms(x, g):
    return x * lax.rsqrt(jnp.mean(x * x, axis=-1, keepdims=True) + RMS_EPS) * g


def _split_bf16(x):
    hi = x.astype(BF16)
    lo = (x - hi.astype(F32)).astype(BF16)
    return hi, lo


def _const_spec(shape):
    nd = len(shape)
    return pl.BlockSpec(shape, lambda *_: (0,) * nd, pipeline_mode=pl.Buffered(1))


def _layer_spec(shape, layer):
    nd = len(shape)
    return pl.BlockSpec((None,) + tuple(shape), lambda *_: (layer,) + (0,) * nd, pipeline_mode=pl.Buffered(1))


def _row_spec(width):
    return pl.BlockSpec((ROW_TILE, width), lambda i: (i, 0))


def _params(n_axes=1):
    return pltpu.CompilerParams(dimension_semantics=("arbitrary",) * n_axes,
                                vmem_limit_bytes=VMEM_LIMIT)


def _ffn_body(n_prompt_tiles, split_x, attn, ple, final, n_cast, *refs):
    refs = list(refs)
    is_prompt = pl.program_id(0) < n_prompt_tiles
    take = lambda n: [refs.pop(0) for _ in range(n)]
    x_refs = take(2 if split_x else 1)
    attn_refs = take(3 if attn else 0)
    g_ref, wg_ref, wu_ref, wd_ref = take(4)
    ple_refs = take(5 if ple else 0)
    final_refs = take(1 if final else 0)
    cast_in = take(n_cast)
    out_refs = take(2 if final else 1)
    cast_out = refs

    for src, dst in zip(cast_in, cast_out):
        dst[...] = src[...].astype(BF16)

    def pick(prompt_ref, sample_ref, rows):
        return jnp.where(is_prompt, prompt_ref[rows, :], sample_ref[rows, :])

    def expand(rows):
        x = pick(*x_refs, rows) if split_x else x_refs[0][rows, :]
        if attn:
            op_ref, os_ref, wo_ref = attn_refs
            x = x + _dot(pick(op_ref, os_ref, rows), wo_ref[...])
        h = _rms(x, g_ref[...]).astype(BF16)
        return x, _dot(h, wg_ref[...]), _dot(h, wu_ref[...])

    def contract(x, gate, up):
        a = (gate * jax.nn.sigmoid(gate) * up).astype(BF16)
        return x + 0.5 * _dot(a, wd_ref[...])

    def embed(rows, x):
        if ple:
            gp_ref, wgate_ref, pp_ref, ps_ref, wproj_ref = ple_refs
            h = _rms(x, gp_ref[...]).astype(BF16)
            x = x + jax.nn.sigmoid(_dot(h, wgate_ref[...])) * _dot(pick(pp_ref, ps_ref, rows).astype(BF16),
                                                                    wproj_ref[...])
        return _rms(x, final_refs[0][...]) if final else x

    group_rows = ROW_TILE // FFN_ROW_GROUPS
    groups = [slice(r * group_rows, (r + 1) * group_rows) for r in range(FFN_ROW_GROUPS)]
    staged = [expand(rows) for rows in groups]
    staged = [contract(*s) for s in staged]
    staged = [embed(rows, x) for rows, x in zip(groups, staged)]
    if not final:
        for rows, x in zip(groups, staged):
            out_refs[0][rows, :] = x
        return
    yp_ref, ys_ref = out_refs

    @pl.when(is_prompt)
    def _():
        for rows, y in zip(groups, staged):
            yp_ref[rows, :] = y

    @pl.when(jnp.logical_not(is_prompt))
    def _():
        for rows, y in zip(groups, staged):
            ys_ref[rows, :] = y


WEIGHT_CAST_CHUNKS = 16


def _ffn(x, layer, gains, weights, n_prompt, attn=None, ple=None, final_gain=None, cast_next=None):
    wg, wu, wd = weights
    npt = n_prompt // ROW_TILE
    prompt_rows = lambda i: (jnp.minimum(i, npt - 1), 0)
    sample_rows = lambda i: (0, 0)
    split_x = isinstance(x, tuple)
    if split_x:
        args = list(x)
        in_specs = [pl.BlockSpec((ROW_TILE, D_MODEL), prompt_rows), pl.BlockSpec((ROW_TILE, D_MODEL), sample_rows)]
        n = n_prompt + x[1].shape[0]
    else:
        args = [x]
        in_specs = [_row_spec(D_MODEL)]
        n = x.shape[0]
    assert n - n_prompt == ROW_TILE
    if attn is not None:
        o_p, o_s, w_out, slot = attn
        args += [o_p, o_s, w_out]
        in_specs += [pl.BlockSpec((ROW_TILE, D_MODEL), prompt_rows), pl.BlockSpec((ROW_TILE, D_MODEL), sample_rows),
                     _layer_spec((D_MODEL, D_MODEL), slot)]
    args += [gains, wg, wu, wd]
    in_specs += [_layer_spec((1, D_MODEL), layer), _const_spec((D_MODEL, D_FF)),
                 _const_spec((D_MODEL, D_FF)), _const_spec((D_FF, D_MODEL))]
    if ple is not None:
        gp, wgate, p_prompt, p_sample, wproj = ple
        args += [gp, wgate, p_prompt, p_sample, wproj]
        in_specs += [_layer_spec((1, D_MODEL), layer), _layer_spec((D_MODEL, D_MODEL), layer),
                     pl.BlockSpec((None, ROW_TILE, PLE_DIM), lambda i: (layer, jnp.minimum(i, npt - 1), 0)),
                     pl.BlockSpec((None, ROW_TILE, PLE_DIM), lambda i: (layer, 0, 0)),
                     _layer_spec((PLE_DIM, D_MODEL), layer)]
    if final_gain is None:
        out_specs = [_row_spec(D_MODEL)]
        out_shape = [jax.ShapeDtypeStruct((n, D_MODEL), F32)]
    else:
        args.append(final_gain)
        in_specs.append(_const_spec((1, D_MODEL)))
        out_specs = [pl.BlockSpec((ROW_TILE, D_MODEL), prompt_rows), pl.BlockSpec((ROW_TILE, D_MODEL), sample_rows)]
        out_shape = [jax.ShapeDtypeStruct((n_prompt, D_MODEL), F32), jax.ShapeDtypeStruct((ROW_TILE, D_MODEL), F32)]
    n_main = len(out_shape)
    if cast_next:
        assert n // ROW_TILE >= WEIGHT_CAST_CHUNKS
        for w, layer_sel in cast_next:
            n_out = w.shape[1] if layer_sel is not None else w.shape[0] * w.shape[1]
            first = (layer_sel or 0) * WEIGHT_CAST_CHUNKS
            rows, cols = n_out // WEIGHT_CAST_CHUNKS, w.shape[2]
            assert rows * WEIGHT_CAST_CHUNKS == n_out and rows % (2 * SUBLANES) == 0
            chunk = lambda i: jnp.minimum(i, WEIGHT_CAST_CHUNKS - 1)
            args.append(w.reshape(-1, cols))
            in_specs.append(pl.BlockSpec((rows, cols), lambda i, chunk=chunk, first=first: (first + chunk(i), 0)))
            out_specs.append(pl.BlockSpec((rows, cols), lambda i, chunk=chunk: (chunk(i), 0)))
            out_shape.append(jax.ShapeDtypeStruct((n_out, cols), BF16))
    outs = pl.pallas_call(
        functools.partial(_ffn_body, npt, split_x, attn is not None, ple is not None, final_gain is not None,
                          len(cast_next) if cast_next else 0),
        grid=(n // ROW_TILE,),
        in_specs=in_specs,
        out_specs=out_specs,
        out_shape=out_shape,
        compiler_params=_params(),
        name="ffn_ple" if ple is not None else "ffn",
    )(*args)
    main = outs[0] if n_main == 1 else tuple(outs[:n_main])
    return main, tuple(outs[n_main:])


def _mixa_body(n_prompt_tiles, dec_batch, dec_seq,
               x_ref, g_ref, win_ref, lng_ref, lnb_ref, ws_ref, bexp_ref, wrow_ref, brow_ref, wout_ref,
               o_ref, vs_ref, v_scr, m_scr):
    i = pl.program_id(0)

    def project(rows):
        return _dot(_rms(x_ref[rows, :], g_ref[...]).astype(BF16), win_ref[...])

    def activate(rows, z):
        z = 0.5 * z * (1.0 + lax.erf(z * (1.0 / math.sqrt(2.0))))
        v = z[:, D_MODEL:]
        mu = jnp.mean(v, axis=-1, keepdims=True)
        vc = v - mu
        var = jnp.mean(vc * vc, axis=-1, keepdims=True)
        v_scr[rows, :] = vc * lax.rsqrt(var + LN_EPS) * lng_ref[...] + lnb_ref[...]
        return z[:, :D_MODEL]

    def back(rows, u):
        o_ref[rows, :] = x_ref[rows, :] + _dot((u * m_scr[rows, :]).astype(BF16), wout_ref[...])

    @pl.when(i < n_prompt_tiles)
    def _():
        r_iota = lax.broadcasted_iota(jnp.int32, (CHUNK, CHUNK), 0)
        c_iota = lax.broadcasted_iota(jnp.int32, (CHUNK, CHUNK), 1)
        w_tril = [jnp.where(c_iota <= r_iota, ws_ref[g], 0.0).astype(BF16) for g in range(A_GROUPS)]
        group_rows = ROW_TILE // MIX_ROW_GROUPS
        groups = [slice(r * group_rows, (r + 1) * group_rows) for r in range(MIX_ROW_GROUPS)]

        def mix(r):
            chunks = range(r * group_rows // CHUNK, (r + 1) * group_rows // CHUNK)
            for g in range(A_GROUPS):
                gs = slice(g * LANES, (g + 1) * LANES)
                rhs = jnp.concatenate([v_scr[c * CHUNK:(c + 1) * CHUNK, gs] for c in chunks], axis=1).astype(BF16)
                mg = _dot(w_tril[g], rhs)
                for k, c in enumerate(chunks):
                    m_scr[c * CHUNK:(c + 1) * CHUNK, gs] = mg[:, k * LANES:(k + 1) * LANES] + bexp_ref[:, gs]

        z = project(groups[0])
        for r in range(MIX_ROW_GROUPS):
            z_next = project(groups[r + 1]) if r + 1 < MIX_ROW_GROUPS else None
            u = activate(groups[r], z)
            mix(r)
            back(groups[r], u)
            z = z_next

    @pl.when(i >= n_prompt_tiles)
    def _():
        rows = slice(0, ROW_TILE)
        u = activate(rows, project(rows))
        for t in range(dec_seq):
            acc = brow_ref[t:t + 1, :] + wrow_ref[t, 0:1, :] * v_scr[0:dec_batch, :]
            for s in range(1, t + 1):
                acc = acc + wrow_ref[t, s:s + 1, :] * v_scr[s * dec_batch:(s + 1) * dec_batch, :]
            m_scr[t * dec_batch:(t + 1) * dec_batch, :] = acc
        vs_ref[...] = v_scr[...]
        back(rows, u)


def _mixer_a(x, layer, slot, g, w_in, ln_g, ln_b, w_s, bexp, wrow, brow, w_out, n_prompt, dec_batch, dec_seq):
    n = x.shape[0]
    assert n - n_prompt == ROW_TILE == dec_batch * dec_seq and dec_batch % SUBLANES == 0
    body = functools.partial(_mixa_body, n_prompt // ROW_TILE, dec_batch, dec_seq)
    return pl.pallas_call(
        body,
        grid=(n // ROW_TILE,),
        in_specs=[_row_spec(D_MODEL), _layer_spec((1, D_MODEL), layer), _layer_spec((D_MODEL, 2 * D_MODEL), slot),
                  _layer_spec((1, D_MODEL), slot), _layer_spec((1, D_MODEL), slot),
                  _layer_spec((A_GROUPS, CHUNK, CHUNK), slot), _const_spec((CHUNK, D_MODEL)),
                  _const_spec((dec_seq, dec_seq, D_MODEL)), _const_spec((dec_seq, D_MODEL)),
                  _layer_spec((D_MODEL, D_MODEL), slot)],
        out_specs=[_row_spec(D_MODEL), pl.BlockSpec((ROW_TILE, D_MODEL), lambda i: (0, 0))],
        out_shape=[jax.ShapeDtypeStruct((n, D_MODEL), F32),
                   jax.ShapeDtypeStruct((ROW_TILE, D_MODEL), F32)],
        scratch_shapes=[pltpu.VMEM((ROW_TILE, D_MODEL), F32), pltpu.VMEM((ROW_TILE, D_MODEL), F32)],
        compiler_params=_params(),
        name="mixer_a",
    )(x, g, w_in, ln_g, ln_b, w_s, bexp, wrow, brow, w_out)


HIST_PAD = 16


def _mixc_body(n_prompt_tiles, tiles_per_seq, dec_batch, dec_seq,
               x_ref, g_ref, win_ref, wgrp_ref, scale_ref, wout_ref, hist_ref,
               o_ref, tail_ref, us_ref, ext_scr, pool_scr):
    i = pl.program_id(0)

    def project(rows):
        return _dot(_rms(x_ref[rows, :], g_ref[...]).astype(BF16), win_ref[...])

    def finish(rows):
        parts = []
        for g in range(len(POOL_WINDOWS)):
            gs = slice(g * C_GROUP_DIM, (g + 1) * C_GROUP_DIM)
            parts.append(_dot(pool_scr[rows, gs].astype(BF16), wgrp_ref[g]))
        mixed = jnp.concatenate(parts, axis=1) * scale_ref[...]
        o_ref[rows, :] = x_ref[rows, :] + _dot(mixed.astype(BF16), wout_ref[...])

    @pl.when(i < n_prompt_tiles)
    def _():
        @pl.when(i % tiles_per_seq == 0)
        def _():
            ext_scr[0:HIST_PAD, :] = jnp.zeros((HIST_PAD, D_MODEL), F32)

        group_rows = ROW_TILE // MIX_ROW_GROUPS

        def pool(r):
            lo = r * group_rows
            pos = (i % tiles_per_seq) * ROW_TILE + lo + lax.broadcasted_iota(jnp.int32, (group_rows, 1), 0)
            for g, win in enumerate(POOL_WINDOWS):
                gs = slice(g * C_GROUP_DIM, (g + 1) * C_GROUP_DIM)
                cur = ext_scr[HIST_PAD + lo:HIST_PAD + lo + group_rows, gs]
                tot = cur
                for d in range(1, win):
                    tot = tot + ext_scr[HIST_PAD + lo - d:HIST_PAD + lo - d + group_rows, gs]
                inv = 1.0 / jnp.minimum(pos + 1, win).astype(F32)
                pool_scr[lo:lo + group_rows, gs] = tot * inv - cur

        for r in range(MIX_ROW_GROUPS):
            if r == 0:
                ext_scr[HIST_PAD:HIST_PAD + group_rows, :] = project(slice(0, group_rows))
            if r + 1 < MIX_ROW_GROUPS:
                nxt = slice((r + 1) * group_rows, (r + 2) * group_rows)
                ext_scr[HIST_PAD + nxt.start:HIST_PAD + nxt.stop, :] = project(nxt)
            pool(r)
            finish(slice(r * group_rows, (r + 1) * group_rows))
        tail = ext_scr[ROW_TILE:, :]
        tail_ref[0] = tail
        ext_scr[0:HIST_PAD, :] = tail

    @pl.when(i >= n_prompt_tiles)
    def _():
        u = project(slice(0, ROW_TILE))
        ext_scr[0:ROW_TILE, :] = u
        us_ref[...] = u
        tail_ref[0] = u[ROW_TILE - HIST_PAD:, :]

        def ext_row(j, gs):
            if j < POOL_HIST:
                return hist_ref[j, :, gs]
            return ext_scr[(j - POOL_HIST) * dec_batch:(j - POOL_HIST + 1) * dec_batch, gs]

        for t in range(dec_seq):
            for g, win in enumerate(POOL_WINDOWS):
                gs = slice(g * C_GROUP_DIM, (g + 1) * C_GROUP_DIM)
                cur = ext_row(POOL_HIST + t, gs)
                tot = cur
                for d in range(1, win):
                    tot = tot + ext_row(POOL_HIST + t - d, gs)
                pool_scr[t * dec_batch:(t + 1) * dec_batch, gs] = tot * (1.0 / win) - cur
        finish(slice(0, ROW_TILE))


def _mixer_c(x, layer, slot, g, w_in, w_grp, scale, w_out, hist_t, n_prompt, seq, dec_batch, dec_seq):
    n = x.shape[0]
    nt = n // ROW_TILE
    assert n - n_prompt == ROW_TILE == dec_batch * dec_seq and seq % ROW_TILE == 0
    assert POOL_HIST <= HIST_PAD and max(POOL_WINDOWS) - 1 <= POOL_HIST
    body = functools.partial(_mixc_body, n_prompt // ROW_TILE, seq // ROW_TILE, dec_batch, dec_seq)
    return pl.pallas_call(
        body,
        grid=(nt,),
        in_specs=[_row_spec(D_MODEL), _layer_spec((1, D_MODEL), layer), _layer_spec((D_MODEL, D_MODEL), slot),
                  _layer_spec((len(POOL_WINDOWS), C_GROUP_DIM, C_GROUP_DIM), slot), _layer_spec((1, D_MODEL), slot),
                  _layer_spec((D_MODEL, D_MODEL), slot), _const_spec((POOL_HIST, dec_batch, D_MODEL))],
        out_specs=[_row_spec(D_MODEL),
                   pl.BlockSpec((1, HIST_PAD, D_MODEL), lambda i: (i, 0, 0)),
                   pl.BlockSpec((ROW_TILE, D_MODEL), lambda i: (0, 0))],
        out_shape=[jax.ShapeDtypeStruct((n, D_MODEL), F32),
                   jax.ShapeDtypeStruct((nt, HIST_PAD, D_MODEL), F32),
                   jax.ShapeDtypeStruct((ROW_TILE, D_MODEL), F32)],
        scratch_shapes=[pltpu.VMEM((HIST_PAD + ROW_TILE, D_MODEL), F32),
                        pltpu.VMEM((ROW_TILE, D_MODEL), F32)],
        compiler_params=_params(),
        name="mixer_c",
    )(x, g, w_in, w_grp, scale, w_out, hist_t)


def _rope(xh, cos_t, sin_t, lane):
    partner = jnp.where(lane < ROT_DIM // 2, pltpu.roll(xh, LANES - ROT_DIM // 2, 1),
                        pltpu.roll(xh, ROT_DIM // 2, 1))
    return xh * cos_t + partner * sin_t


def _qkv_body(n_prompt_tiles, x_ref, g_ref, w_ref, cos_ref, sin_ref,
              q_ref, kp_ref, vp_ref, ks_ref, vs_ref):
    i = pl.program_id(0)
    group_rows = ROW_TILE // MIX_ROW_GROUPS
    groups = [slice(r * group_rows, (r + 1) * group_rows) for r in range(MIX_ROW_GROUPS)]
    lane = lax.broadcasted_iota(jnp.int32, (group_rows, LANES), 1)

    def project(rows):
        return _dot(_rms(x_ref[rows, :], g_ref[...]).astype(BF16), w_ref[...])

    def emit(k_ref, v_ref):
        def rotate(rows, qkv):
            cos_t = cos_ref[rows, :]
            sin_t = sin_ref[rows, :]
            for hd in range(N_HEADS):
                cols = slice(hd * HEAD_DIM, (hd + 1) * HEAD_DIM)
                q_ref[rows, cols] = _rope(qkv[:, cols], cos_t, sin_t, lane)
                off = D_MODEL + hd * HEAD_DIM
                k_ref[rows, cols] = _rope(qkv[:, off:off + HEAD_DIM], cos_t, sin_t, lane)
            v_ref[rows, :] = qkv[:, 2 * D_MODEL:]

        qkv = project(groups[0])
        for r in range(MIX_ROW_GROUPS):
            qkv_next = project(groups[r + 1]) if r + 1 < MIX_ROW_GROUPS else None
            rotate(groups[r], qkv)
            qkv = qkv_next

    pl.when(i < n_prompt_tiles)(functools.partial(emit, kp_ref, vp_ref))
    pl.when(i >= n_prompt_tiles)(functools.partial(emit, ks_ref, vs_ref))


def _qkv(x, layer, slot, g, w_qkv, cos_t, sin_t, n_prompt, seq):
    n = x.shape[0]
    npt = n_prompt // ROW_TILE
    tps = seq // ROW_TILE
    assert n - n_prompt == ROW_TILE and cos_t.shape[0] == seq + ROW_TILE
    prompt_spec = pl.BlockSpec((ROW_TILE, D_MODEL), lambda i: (jnp.minimum(i, npt - 1), 0))
    sample_spec = pl.BlockSpec((ROW_TILE, D_MODEL), lambda i: (0, 0))
    rope_spec = pl.BlockSpec((ROW_TILE, LANES), lambda i: (jnp.where(i < npt, i % tps, tps), 0))
    return pl.pallas_call(
        functools.partial(_qkv_body, npt),
        grid=(n // ROW_TILE,),
        in_specs=[_row_spec(D_MODEL), _layer_spec((1, D_MODEL), layer), _layer_spec((D_MODEL, 3 * D_MODEL), slot),
                  rope_spec, rope_spec],
        out_specs=[_row_spec(D_MODEL), prompt_spec, prompt_spec, sample_spec, sample_spec],
        out_shape=[jax.ShapeDtypeStruct((n, D_MODEL), F32),
                   jax.ShapeDtypeStruct((n_prompt, D_MODEL), F32),
                   jax.ShapeDtypeStruct((n_prompt, D_MODEL), F32),
                   jax.ShapeDtypeStruct((ROW_TILE, D_MODEL), F32),
                   jax.ShapeDtypeStruct((ROW_TILE, D_MODEL), F32)],
        compiler_params=_params(),
        name="moba_qkv",
    )(x, g, w_qkv, cos_t, sin_t)


def _top_k_mask(gate, valid, k, axis):
    idx = lax.broadcasted_iota(jnp.int32, gate.shape, axis)
    size = gate.shape[axis]
    g = jnp.where(valid, gate, -jnp.inf)
    chosen = jnp.zeros(gate.shape, dtype=jnp.bool_)
    for _ in range(k):
        mx = jnp.max(g, axis=axis, keepdims=True)
        first = jnp.min(jnp.where(g == mx, idx, size), axis=axis, keepdims=True)
        pick = (idx == first) & (mx > -jnp.inf)
        chosen = chosen | pick
        g = jnp.where(pick, -jnp.inf, g)
    return chosen


KV_CHUNK = 4
PREP_BLOCKS = 4
SUM_ROWS = 16


def _column_fold(x, op):
    return op(x.reshape(x.shape[0] // SUBLANES, SUBLANES, x.shape[1]), axis=0)


def _pattn_body(n_blocks, q_ref, k_ref, v_ref, o_ref, kaug_scr, vt_scr, qaug_scr, kmean_scr, s_scr):
    qp = pl.program_id(2)
    exp2_scale = (HEAD_DIM ** -0.5) * math.log2(math.e)
    chunk = KV_CHUNK * MOBA_BLOCK

    @pl.when(qp == 0)
    def _():
        lane = lax.broadcasted_iota(jnp.int32, (MOBA_BLOCK, LANES), 1)
        sum_rows = jnp.where(lax.broadcasted_iota(jnp.int32, (SUM_ROWS, MOBA_BLOCK), 0) == 0, 1.0, 0.0).astype(BF16)

        def fill(j, carry):
            rows = pl.ds(pl.multiple_of(j * MOBA_BLOCK, MOBA_BLOCK), MOBA_BLOCK)
            kj = k_ref[rows, :]
            kaug_scr[rows, 0:HEAD_DIM] = kj.astype(BF16)
            kaug_scr[rows, HEAD_DIM:] = jnp.where(lane == j, 1.0, 0.0).astype(BF16)
            vt_scr[j] = jnp.concatenate([v_ref[rows, :].T.astype(BF16), sum_rows], axis=0)
            kmean_scr[pl.ds(j, 1), :] = jnp.mean(kj, axis=0, keepdims=True)
            return carry

        lax.fori_loop(0, n_blocks, fill, 0, unroll=4)
        m_hi, m_lo = _split_bf16(kmean_scr[...])
        width = PREP_BLOCKS * MOBA_BLOCK
        blk = lax.broadcasted_iota(jnp.int32, (n_blocks, width), 0)
        col_blk = lax.broadcasted_iota(jnp.int32, (n_blocks, width), 1) // MOBA_BLOCK
        pad = jnp.zeros((LANES - n_blocks, MOBA_BLOCK), BF16)

        def prep(jj, carry):
            rows = pl.ds(pl.multiple_of(jj * width, width), width)
            qt = q_ref[rows, :].T
            qt_hi, qt_lo = _split_bf16(qt)
            gate = _dot(m_hi, qt_hi) + _dot(m_lo, qt_hi) + _dot(m_hi, qt_lo)
            allowed = _top_k_mask(gate, blk < col_blk + jj * PREP_BLOCKS, MOBA_TOP_K, 0)
            bias = jnp.where(allowed, 0.0, NEG_BIG).astype(BF16)
            qt_op = (qt * exp2_scale).astype(BF16)
            for i in range(PREP_BLOCKS):
                cols = slice(i * MOBA_BLOCK, (i + 1) * MOBA_BLOCK)
                qaug_scr[jj * PREP_BLOCKS + i] = jnp.concatenate([qt_op[:, cols], bias[:, cols], pad], axis=0)
            return carry

        lax.fori_loop(0, n_blocks // PREP_BLOCKS, prep, 0, unroll=2)

    blk_a = 2 * qp
    q_aug = jnp.concatenate([qaug_scr[blk_a], qaug_scr[blk_a + 1]], axis=1)
    own_a = pl.ds(pl.multiple_of(blk_a * MOBA_BLOCK, MOBA_BLOCK), MOBA_BLOCK)
    own_b = pl.ds(pl.multiple_of((blk_a + 1) * MOBA_BLOCK, MOBA_BLOCK), MOBA_BLOCK)

    def attend(k_chunks):
        def score(c):
            s_scr[c % 2] = _dot(kaug_scr[c * chunk:(c + 1) * chunk, :], q_aug)

        def values(c, pb):
            pv = _dot(vt_scr[c * KV_CHUNK], pb[0:MOBA_BLOCK, :])
            for j in range(1, KV_CHUNK):
                pv = pv + _dot(vt_scr[c * KV_CHUNK + j], pb[j * MOBA_BLOCK:(j + 1) * MOBA_BLOCK, :])
            return pv

        score(0)
        if k_chunks > 1:
            score(1)
        s_own = jnp.concatenate(
            [_dot(kaug_scr[own_a, 0:HEAD_DIM], q_aug[0:HEAD_DIM, 0:MOBA_BLOCK]),
             _dot(kaug_scr[own_b, 0:HEAD_DIM], q_aug[0:HEAD_DIM, MOBA_BLOCK:])], axis=1)
        key = lax.broadcasted_iota(jnp.int32, s_own.shape, 0)
        qry = lax.broadcasted_iota(jnp.int32, s_own.shape, 1) % MOBA_BLOCK
        s_own = jnp.where(key <= qry, s_own, NEG_BIG)
        s = s_scr[0]
        m8 = jnp.maximum(_column_fold(s_own, jnp.max), _column_fold(s, jnp.max))
        m = jnp.max(m8, axis=0, keepdims=True)
        pb_own = jnp.exp2(s_own - m).astype(BF16)
        acc = jnp.concatenate([_dot(vt_scr[blk_a], pb_own[:, 0:MOBA_BLOCK]),
                               _dot(vt_scr[blk_a + 1], pb_own[:, MOBA_BLOCK:])], axis=1)
        acc = acc + values(0, jnp.exp2(s - m).astype(BF16))
        for c in range(1, k_chunks):
            if c + 1 < k_chunks:
                score(c + 1)
            s = s_scr[c % 2]
            m_new = jnp.maximum(m, jnp.max(_column_fold(s, jnp.max), axis=0, keepdims=True))
            acc = jnp.exp2(m - m_new) * acc + values(c, jnp.exp2(s - m_new).astype(BF16))
            m = m_new
        o_ref[...] = (acc[0:HEAD_DIM, :] * (1.0 / acc[HEAD_DIM:HEAD_DIM + 1, :])).T.astype(o_ref.dtype)

    n_chunks = (blk_a + 1 + KV_CHUNK - 1) // KV_CHUNK
    for k_chunks in range(1, n_blocks // KV_CHUNK + 1):
        pl.when(n_chunks == k_chunks)(functools.partial(attend, k_chunks))


def _prompt_attention(q_all, k_p, v_p, batch, seq):
    n_blocks = seq // MOBA_BLOCK
    assert seq % MOBA_BLOCK == 0 and n_blocks <= LANES
    assert n_blocks % KV_CHUNK == 0 and n_blocks % PREP_BLOCKS == 0
    assert n_blocks % (2 * SUBLANES) == 0
    n_pairs = n_blocks // 2
    pairspec = pl.BlockSpec((2 * MOBA_BLOCK, HEAD_DIM), lambda b, h, i: (b * n_pairs + i, h))
    seqspec = pl.BlockSpec((seq, HEAD_DIM), lambda b, h, i: (b, h))
    return pl.pallas_call(
        functools.partial(_pattn_body, n_blocks),
        grid=(batch, N_HEADS, n_pairs),
        in_specs=[seqspec, seqspec, seqspec],
        out_specs=pairspec,
        out_shape=jax.ShapeDtypeStruct((batch * seq, D_MODEL), BF16),
        scratch_shapes=[pltpu.VMEM((seq, 2 * LANES), BF16),
                        pltpu.VMEM((n_blocks, HEAD_DIM + SUM_ROWS, MOBA_BLOCK), BF16),
                        pltpu.VMEM((n_blocks, 2 * LANES, MOBA_BLOCK), BF16),
                        pltpu.VMEM((n_blocks, HEAD_DIM), F32),
                        pltpu.VMEM((2, KV_CHUNK * MOBA_BLOCK, 2 * MOBA_BLOCK), F32)],
        compiler_params=_params(3),
        name="moba_prompt_attn",
    )(q_all, k_p, v_p)


def _sattn_body(n_pages, dec_seq, pt_ref, q_ref, kn_ref, vn_ref, ck_hbm, cv_hbm, o_ref,
                kbuf, vbuf, sem, s_scr, kmean_scr):
    n = pl.program_id(0)
    n_seq = pl.num_programs(0)
    slot = n % 2
    scale = HEAD_DIM ** -0.5
    pages_per_block = MOBA_BLOCK // PAGE_SIZE
    n_blocks = n_pages // pages_per_block

    def page_copy(hbm, buf, seq, sl, p, kind):
        return pltpu.make_async_copy(hbm.at[pt_ref[seq * n_pages + p]], buf.at[sl, p], sem.at[kind, sl])

    def start_all(seq, sl):
        for p in range(n_pages):
            page_copy(ck_hbm, kbuf, seq, sl, p, 0).start()
            page_copy(cv_hbm, vbuf, seq, sl, p, 1).start()

    @pl.when(n == 0)
    def _():
        start_all(0, 0)

    @pl.when(n + 1 < n_seq)
    def _():
        start_all(n + 1, 1 - slot)

    for p in range(n_pages):
        page_copy(ck_hbm, kbuf, n, slot, p, 0).wait()
        page_copy(cv_hbm, vbuf, n, slot, p, 1).wait()

    def page_rows(buf, p):
        return jnp.concatenate([buf[slot, p, pl.ds(hd, PAGE_SIZE, stride=N_HEADS), :]
                                for hd in range(N_HEADS)], axis=1)

    pad_rows = jnp.zeros((PAGE_SIZE - dec_seq, D_MODEL), F32)
    q_pad = jnp.concatenate([q_ref[0], pad_rows], axis=0)
    row_t = lax.broadcasted_iota(jnp.int32, (PAGE_SIZE, LANES), 0)
    col_c = lax.broadcasted_iota(jnp.int32, (PAGE_SIZE, LANES), 1)
    pick = jnp.where((col_c % dec_seq == row_t) & (col_c < N_HEADS * dec_seq), 1.0, 0.0).astype(BF16)
    head_r = lax.broadcasted_iota(jnp.int32, (D_MODEL, LANES), 0) // HEAD_DIM
    head_c = lax.broadcasted_iota(jnp.int32, (D_MODEL, LANES), 1) // dec_seq
    same_head = head_r == head_c
    q_hi, q_lo = _split_bf16(q_pad)
    qbd_hi = jnp.where(same_head, _dot_t0(q_hi, pick), 0.0).astype(BF16)
    qbd_lo = jnp.where(same_head, _dot_t0(q_lo, pick), 0.0).astype(BF16)

    kmean_scr[...] = jnp.zeros(kmean_scr.shape, F32)
    for b in range(n_blocks):
        ksum = None
        for r in range(pages_per_block):
            p = b * pages_per_block + r
            kp = page_rows(kbuf, p)
            s_scr[p * PAGE_SIZE:(p + 1) * PAGE_SIZE, :] = _dot(kp.astype(BF16), qbd_hi) * scale
            part = jnp.sum(kp, axis=0, keepdims=True)
            ksum = part if ksum is None else ksum + part
        kmean_scr[b:b + 1, :] = ksum * (1.0 / MOBA_BLOCK)
    m_hi, m_lo = _split_bf16(kmean_scr[...])
    gate = _dot(m_hi, qbd_hi) + _dot(m_lo, qbd_hi) + _dot(m_hi, qbd_lo)
    blk = lax.broadcasted_iota(jnp.int32, gate.shape, 0)
    allowed = _top_k_mask(gate, blk < n_blocks, min(MOBA_TOP_K, n_blocks), 0)
    bias = jnp.where(allowed, 0.0, NEG_BIG)

    k_pad = jnp.concatenate([kn_ref[0], pad_rows], axis=0)
    v_pad = jnp.concatenate([vn_ref[0], pad_rows], axis=0)
    s_new = _dot(k_pad.astype(BF16), qbd_hi) * scale
    s_new = jnp.where((row_t <= col_c % dec_seq) & (row_t < dec_seq), s_new, NEG_BIG)

    m = jnp.max(s_new, axis=0, keepdims=True)
    for b in range(n_blocks):
        rows = slice(b * MOBA_BLOCK, (b + 1) * MOBA_BLOCK)
        sb = s_scr[rows, :] + bias[b:b + 1, :]
        s_scr[rows, :] = sb
        m = jnp.maximum(m, jnp.max(sb, axis=0, keepdims=True))
    p_new = jnp.exp(s_new - m)
    l = jnp.sum(p_new, axis=0, keepdims=True)
    for p in range(n_pages):
        rows = slice(p * PAGE_SIZE, (p + 1) * PAGE_SIZE)
        pp = jnp.exp(s_scr[rows, :] - m)
        s_scr[rows, :] = pp
        l = l + jnp.sum(pp, axis=0, keepdims=True)
    inv = 1.0 / l
    used = slice(0, N_HEADS * dec_seq)
    out = _dot_t0((p_new * inv)[:, used].astype(BF16), v_pad.astype(BF16))
    for p in range(n_pages):
        rows = slice(p * PAGE_SIZE, (p + 1) * PAGE_SIZE)
        out = out + _dot_t0((s_scr[rows, :] * inv)[:, used].astype(BF16), page_rows(vbuf, p).astype(BF16))
    out_c = lax.broadcasted_iota(jnp.int32, out.shape, 0)
    out_h = lax.broadcasted_iota(jnp.int32, out.shape, 1) // HEAD_DIM
    for t in range(dec_seq):
        keep = (out_c % dec_seq == t) & (out_c // dec_seq == out_h)
        o_ref[0, t:t + 1, :] = jnp.sum(jnp.where(keep, out, 0.0), axis=0, keepdims=True)


def _sample_attention(page_table, q_s, k_s, v_s, cache_k, cache_v):
    n_seq, n_pages = page_table.shape
    dec_seq = q_s.shape[1]
    assert n_pages % (MOBA_BLOCK // PAGE_SIZE) == 0 and N_HEADS * dec_seq <= LANES and dec_seq <= SUBLANES
    row_spec = pl.BlockSpec((1, dec_seq, D_MODEL), lambda n, pt: (n, 0, 0))
    any_spec = pl.BlockSpec(memory_space=pl.ANY)
    grid_spec = pltpu.PrefetchScalarGridSpec(
        num_scalar_prefetch=1,
        grid=(n_seq,),
        in_specs=[row_spec, row_spec, row_spec, any_spec, any_spec],
        out_specs=row_spec,
        scratch_shapes=[pltpu.VMEM((2, n_pages, PAGE_SIZE * N_HEADS, HEAD_DIM), F32),
                        pltpu.VMEM((2, n_pages, PAGE_SIZE * N_HEADS, HEAD_DIM), F32),
                        pltpu.SemaphoreType.DMA((2, 2)),
                        pltpu.VMEM((n_pages * PAGE_SIZE, LANES), F32),
                        pltpu.VMEM((2 * SUBLANES, D_MODEL), F32)],
    )
    return pl.pallas_call(
        functools.partial(_sattn_body, n_pages, dec_seq),
        grid_spec=grid_spec,
        out_shape=jax.ShapeDtypeStruct((n_seq, dec_seq, D_MODEL), F32),
        compiler_params=_params(),
        name="moba_sample_attn",
    )(page_table.reshape(-1), q_s, k_s, v_s, cache_k, cache_v)


def _rope_tables(seq, past_len, dec_batch, dec_seq):
    half = ROT_DIM // 2
    pos = np.concatenate([np.arange(seq), past_len + np.repeat(np.arange(dec_seq), dec_batch)]).astype(np.float64)
    inv_freq = ROPE_THETA ** (-np.arange(half, dtype=np.float64) / half)
    ang = pos[:, None] * inv_freq[None, :]
    cos, sin = np.cos(ang), np.sin(ang)
    ones = np.ones((pos.shape[0], LANES - ROT_DIM))
    cos_t = np.concatenate([cos, cos, ones], axis=1).astype(np.float32)
    sin_t = np.concatenate([-sin, sin, 0.0 * ones], axis=1).astype(np.float32)
    return jnp.asarray(cos_t), jnp.asarray(sin_t)


def _to_time_major(a):
    return jnp.swapaxes(a, 0, 1).reshape((a.shape[0] * a.shape[1],) + a.shape[2:])


def _from_time_major(a, dec_batch, dec_seq):
    return jnp.swapaxes(a.reshape((dec_seq, dec_batch) + a.shape[1:]), 0, 1)


def kernel(x_prompt, x_sample, cache_k, cache_v, state_pool, page_table, p_prompt, p_sample, norm_ffn1, norm_mix, norm_ffn2, norm_ple, norm_final, ffn1_w_gate, ffn1_w_up, ffn1_w_down, ffn2_w_gate, ffn2_w_up, ffn2_w_down, ple_w_gate, ple_w_proj, a_w_in, a_ln_g, a_ln_b, a_w_s, a_b_s, a_w_out, b_w_qkv, b_w_out, c_w_in, c_w_grp, c_scale, c_w_out):
    batch, seq, d = x_prompt.shape
    dec_batch, dec_seq, _ = x_sample.shape
    n_prompt = batch * seq
    n_sample = dec_batch * dec_seq
    n_pages = page_table.shape[1]
    past_len = n_pages * PAGE_SIZE
    assert d == D_MODEL and n_sample == ROW_TILE and n_prompt % ROW_TILE == 0
    assert past_len % MOBA_BLOCK == 0

    x = (x_prompt.reshape(n_prompt, d), _to_time_major(x_sample))
    p_rows_prompt = p_prompt.reshape(DEPTH, n_prompt, PLE_DIM)
    p_rows_sample = jnp.swapaxes(p_sample, 1, 2).reshape(DEPTH, n_sample, PLE_DIM)
    rows = lambda v: v.reshape(v.shape[0], 1, v.shape[1])
    bf = lambda w: w.astype(BF16)
    g_ffn1, g_mix, g_ffn2, g_ple = rows(norm_ffn1), rows(norm_mix), rows(norm_ffn2), rows(norm_ple)
    w1 = (ffn1_w_gate, ffn1_w_up, ffn1_w_down)
    w2 = (ffn2_w_gate, ffn2_w_up, ffn2_w_down)
    weights = tuple(bf(w[0]) for w in w1)
    small = [ple_w_gate, ple_w_proj, a_w_in, a_w_out, b_w_qkv, b_w_out, c_w_in,
             c_w_grp.reshape(c_w_grp.shape[0], -1, C_GROUP_DIM), c_w_out]

    k_p, v_p, k_s, v_s, pool_p, pool_s, chunk_v = [], [], [], [], [], [], []
    for i in range(DEPTH):
        kind, j = LAYER_MIXERS[i], LAYER_SLOT[i]
        casts = [(w, i) for w in w2] + ([(w, None) for w in small] if i == 0 else [])
        x, narrowed = _ffn(x, i, g_ffn1, weights, n_prompt, cast_next=casts)
        weights = narrowed[:3]
        if i == 0:
            (ple_gate, ple_proj, a_in, a_out, w_qkv, w_attn_out, c_in, c_grp, c_out) = [
                nb.reshape(w.shape) for nb, w in zip(narrowed[3:], small)]
            ple = (g_ple, ple_gate, p_rows_prompt, p_rows_sample, ple_proj)
            a_w = (a_in, rows(a_ln_g), rows(a_ln_b), a_w_s)
            c_w = (c_in, c_grp.reshape(c_w_grp.shape), rows(c_scale), c_out)
        attn = None
        if kind == 0:
            bexp = jnp.repeat(a_b_s[j].T, LANES, axis=1)
            wrow = jnp.repeat(jnp.transpose(a_w_s[j][:, :dec_seq, :dec_seq], (1, 2, 0)), LANES, axis=2)
            brow = jnp.repeat(a_b_s[j][:, :dec_seq].T, LANES, axis=1)
            x, vs = _mixer_a(x, i, j, g_mix, *a_w, bexp, wrow, brow, a_out, n_prompt, dec_batch, dec_seq)
            chunk_v.append(_from_time_major(vs, dec_batch, dec_seq))
        elif kind == 1:
            cos_t, sin_t = _rope_tables(seq, past_len, dec_batch, dec_seq)
            q_all, kp, vp, ks, vs = _qkv(x, i, j, g_mix, w_qkv, cos_t, sin_t, n_prompt, seq)
            o_p = _prompt_attention(q_all, kp, vp, batch, seq)
            q_s = _from_time_major(q_all[n_prompt:], dec_batch, dec_seq)
            ks_n = _from_time_major(ks, dec_batch, dec_seq)
            vs_n = _from_time_major(vs, dec_batch, dec_seq)
            o_s = _sample_attention(page_table, q_s, ks_n, vs_n,
                                    cache_k[j].reshape(-1, PAGE_SIZE * N_HEADS, HEAD_DIM),
                                    cache_v[j].reshape(-1, PAGE_SIZE * N_HEADS, HEAD_DIM))
            attn = (o_p, _to_time_major(o_s).astype(BF16), w_attn_out, j)
            k_p.append(kp.reshape(batch, seq, N_HEADS, HEAD_DIM))
            v_p.append(vp.reshape(batch, seq, N_HEADS, HEAD_DIM))
            k_s.append(ks_n.reshape(dec_batch, dec_seq, N_HEADS, HEAD_DIM))
            v_s.append(vs_n.reshape(dec_batch, dec_seq, N_HEADS, HEAD_DIM))
        else:
            hist_t = jnp.swapaxes(state_pool[j], 0, 1)
            x, tails, us = _mixer_c(x, i, j, g_mix, *c_w, hist_t, n_prompt, seq, dec_batch, dec_seq)
            tps = seq // ROW_TILE
            pool_p.append(tails[tps - 1:batch * tps:tps, HIST_PAD - POOL_HIST:, :])
            u_new = _from_time_major(us, dec_batch, dec_seq)
            pool_s.append(jnp.concatenate([state_pool[j], u_new], axis=1)[:, -POOL_HIST:])
        last = i + 1 == DEPTH
        x, weights = _ffn(x, i, g_ffn2, weights, n_prompt, attn=attn, ple=ple,
                          final_gain=norm_final.reshape(1, -1) if last else None,
                          cast_next=None if last else [(w, i + 1) for w in w1])
    y_prompt, y_sample = x
    y_prompt = y_prompt.reshape(batch, seq, d)
    y_sample = _from_time_major(y_sample, dec_batch, dec_seq)
    return (y_prompt, y_sample, jnp.stack(k_p), jnp.stack(v_p), jnp.stack(k_s), jnp.stack(v_s),
            jnp.stack(pool_p), jnp.stack(pool_s), jnp.stack(chunk_v))
```
